```python
import jax
import jax.numpy as jnp
from jax import lax
import numpy as np

D_MODEL = 1024
BATCH = 2
SEQ = 16384
DEPTH = 2

GRID_W = 64
CTX_LEN = 256
HEAD_DIM = 64
ATT_WIDTH = D_MODEL // 2
ATT_HEADS = ATT_WIDTH // HEAD_DIM
LRU_WIDTH = D_MODEL // 2
LRU_BLOCKS = 8
LRU_BLOCK = LRU_WIDTH // LRU_BLOCKS
LRU_CONV = 4
LRU_PAD = (2, 1)
LRU_C = 8.0
IN_WIDTH_EVEN = 3 * ATT_WIDTH + 2 * LRU_WIDTH
WIN_ROWS_MAX = 8
WIN_COLS = 16
Q_ROWS = 2
SC_WIDTH = D_MODEL
SC_CONV = 3
SC_PAD = (1, 1)
FFN_HIDDEN = ((8 * D_MODEL + 3 * 256 - 1) // (3 * 256)) * 256
N_EVEN = (DEPTH + 1) // 2
N_ODD = DEPTH // 2
EPS = 1e-6

kernel_name = "hybrid_na_rglru_shortconv_dit"


def rmsnorm(x, g):
    xf = x.astype(jnp.float32)
    y = xf * lax.rsqrt(jnp.mean(xf * xf, axis=-1, keepdims=True) + EPS)
    return (y * g.astype(jnp.float32)).astype(x.dtype)


def adaln(cond, w, b):
    m = jax.nn.silu(cond) @ w + b
    return jnp.split(m[..., None, :], 6, axis=-1)


def modulate(x, shift, scale):
    return x * (1 + scale) + shift


def dwconv(x, w, pad):
    ch = x.shape[-1]
    return lax.conv_general_dilated(x, w[:, None, :].astype(x.dtype), window_strides=(1,),
                                    padding=[pad], dimension_numbers=("NWC", "WIO", "NWC"),
                                    feature_group_count=ch)


def swiglu(h, w_gu, w_down):
    gt, up = jnp.split(h @ w_gu, 2, axis=-1)
    return (jax.nn.silu(gt) * up) @ w_down


def linear_scan(a, b, h0, reverse):
    if reverse:
        b = b.at[:, -1].add(a[:, -1] * h0)
    else:
        b = b.at[:, 0].add(a[:, 0] * h0)

    def combine(l, r):
        return l[0] * r[0], r[0] * l[1] + r[1]

    _, h = lax.associative_scan(combine, (a, b), reverse=reverse, axis=1)
    return h


def rglru_coeffs(xc, w_r, b_r, w_i, b_i, lam):
    bsz, s, _ = xc.shape
    xb = xc.reshape(bsz, s, LRU_BLOCKS, LRU_BLOCK)
    f32 = jnp.float32
    r = jax.nn.sigmoid(jnp.einsum("bsgi,gij->bsgj", xb, w_r.astype(f32)).reshape(bsz, s, LRU_WIDTH) + b_r.astype(f32))
    i = jax.nn.sigmoid(jnp.einsum("bsgi,gij->bsgj", xb, w_i.astype(f32)).reshape(bsz, s, LRU_WIDTH) + b_i.astype(f32))
    log_a = -LRU_C * r * jax.nn.softplus(-lam.astype(f32))
    return jnp.exp(log_a), jnp.sqrt(-jnp.expm1(2.0 * log_a)) * (i * xc)


def bidir_rglru(xl, xc, w_r, b_r, w_i, b_i, lam, ctx_out):
    xl32 = xl.astype(jnp.float32)
    xc32 = xc.astype(jnp.float32)
    lat, con = [], []
    for d, rev in enumerate((False, True)):
        a_c, u_c = rglru_coeffs(xc32, w_r[d], b_r[d], w_i[d], b_i[d], lam[d])
        h_c = linear_scan(a_c, u_c, jnp.zeros_like(xc32[:, 0]), rev)
        h0 = h_c[:, 0] if rev else h_c[:, -1]
        a_l, u_l = rglru_coeffs(xl32, w_r[d], b_r[d], w_i[d], b_i[d], lam[d])
        lat.append(linear_scan(a_l, u_l, h0, rev))
        con.append(h_c)
    yl = (lat[0] + lat[1]).astype(xl.dtype)
    yc = (con[0] + con[1]).astype(xc.dtype) if ctx_out else None
    return yl, yc


def neighbourhood_attention(q, k, v, k_ctx, v_ctx, rpb):
    bsz, s, nh, dh = q.shape
    rows = s // GRID_W
    wr = min(WIN_ROWS_MAX, rows)
    qg = (q * (dh ** -0.5)).reshape(bsz, rows, GRID_W, nh, dh)
    kg = k.reshape(bsz, rows, GRID_W, nh, dh)
    vg = v.reshape(bsz, rows, GRID_W, nh, dh)
    row_start = jnp.clip(jnp.arange(rows) - wr // 2, 0, rows - wr)
    row_idx = row_start[:, None] + jnp.arange(wr)
    col_start = jnp.clip(jnp.arange(GRID_W) - WIN_COLS // 2, 0, GRID_W - WIN_COLS)
    col_idx = col_start[:, None] + jnp.arange(WIN_COLS)
    dc = col_idx - jnp.arange(GRID_W)[:, None] + (WIN_COLS - 1)
    n_keys_win = wr * WIN_COLS

    def block(i):
        r0 = i * Q_ROWS
        qb = lax.dynamic_slice_in_dim(qg, r0, Q_ROWS, axis=1)
        ridx = lax.dynamic_slice_in_dim(row_idx, r0, Q_ROWS, axis=0)
        kw = jnp.take(jnp.take(kg, ridx, axis=1), col_idx, axis=3)
        vw = jnp.take(jnp.take(vg, ridx, axis=1), col_idx, axis=3)
        dr = ridx - (r0 + jnp.arange(Q_ROWS))[:, None] + (WIN_ROWS_MAX - 1)
        bias = rpb[:, dr[:, None, :, None], dc[None, :, None, :]]
        s_win = jnp.einsum("bqjhd,bqrjchd->bhqjrc", qb, kw) + bias[None]
        s_ctx = jnp.einsum("bqjhd,bnhd->bhqjn", qb, k_ctx)
        logits = jnp.concatenate([s_win.reshape(s_win.shape[:4] + (n_keys_win,)), s_ctx], axis=-1)
        p = jax.nn.softmax(logits.astype(jnp.float32), axis=-1).astype(v.dtype)
        p_win = p[..., :n_keys_win].reshape(s_win.shape)
        p_ctx = p[..., n_keys_win:]
        return (jnp.einsum("bhqjrc,bqrjchd->bqjhd", p_win, vw)
                + jnp.einsum("bhqjn,bnhd->bqjhd", p_ctx, v_ctx))

    out = lax.map(block, jnp.arange(rows // Q_ROWS))
    return jnp.moveaxis(out, 0, 1).reshape(bsz, s, nh * dh)


def context_attention(q, k, v):
    s = jnp.einsum("bnhd,bmhd->bhnm", q, k) * (q.shape[-1] ** -0.5)
    p = jax.nn.softmax(s.astype(jnp.float32), axis=-1).astype(v.dtype)
    out = jnp.einsum("bhnm,bmhd->bnhd", p, v)
    return out.reshape(out.shape[0], out.shape[1], -1)


def hybrid_mixer(hl, hc, w_in, w_out, conv_w, conv_b, lru_wr, lru_br, lru_wi, lru_bi, lru_lam, rpb, ctx_out):
    splits = [ATT_WIDTH, 2 * ATT_WIDTH, 3 * ATT_WIDTH, 3 * ATT_WIDTH + LRU_WIDTH]

    def heads(t):
        return t.reshape(t.shape[0], t.shape[1], ATT_HEADS, HEAD_DIM)

    q, k, v, xr, gr = jnp.split(hl @ w_in, splits, axis=-1)
    w_q, w_k, w_v, w_x, w_g = jnp.split(w_in, splits, axis=1)
    kc, vc = heads(hc @ w_k), heads(hc @ w_v)
    att = neighbourhood_attention(heads(q), heads(k), heads(v), kc, vc, rpb)
    xr = dwconv(xr, conv_w, LRU_PAD) + conv_b
    xc = dwconv(hc @ w_x, conv_w, LRU_PAD) + conv_b
    rec, rec_c = bidir_rglru(xr, xc, lru_wr, lru_br, lru_wi, lru_bi, lru_lam, ctx_out)
    y = jnp.concatenate([att, rec * jax.nn.gelu(gr)], axis=-1) @ w_out
    if not ctx_out:
        return y, None
    att_c = context_attention(heads(hc @ w_q), kc, vc)
    yc = jnp.concatenate([att_c, rec_c * jax.nn.gelu(hc @ w_g)], axis=-1) @ w_out
    return y, yc


def short_conv_mixer(h, w_in, w_conv, w_out):
    bg, cg, xv = jnp.split(h @ w_in, 3, axis=-1)
    return (bg * dwconv(cg * xv, w_conv, SC_PAD)) @ w_out


def setup_inputs(seed: int = 0) -> dict:
    key = jax.random.key(seed)
    ks = iter(jax.random.split(key, 32))
    D = D_MODEL
    f32 = jnp.float32

    def nrm(shape, s):
        return s * jax.random.normal(next(ks), shape, f32)

    u = jax.random.uniform(next(ks), (N_EVEN, 2, LRU_WIDTH), f32, minval=0.9, maxval=0.999)
    p = u ** (1.0 / LRU_C)
    lam = jnp.log(p) - jnp.log1p(-p)
    return {
        "x": nrm((BATCH, SEQ, D), 1.0),
        "c": nrm((BATCH, D), 1.0),
        "ctx": nrm((BATCH, CTX_LEN, D), 1.0),
        "c_ctx": nrm((D,), 1.0),
        "mod_w": nrm((DEPTH, D, 6 * D), D ** -0.5),
        "mod_b": nrm((DEPTH, 6 * D), 0.02),
        "norm1_g": 1.0 + nrm((DEPTH, D), 0.05),
        "norm2_g": 1.0 + nrm((DEPTH, D), 0.05),
        "ffn_w_gu": nrm((DEPTH, D, 2 * FFN_HIDDEN), D ** -0.5),
        "ffn_w_down": nrm((DEPTH, FFN_HIDDEN, D), FFN_HIDDEN ** -0.5),
        "e_w_in": nrm((N_EVEN, D, IN_WIDTH_EVEN), D ** -0.5),
        "e_w_out": nrm((N_EVEN, ATT_WIDTH + LRU_WIDTH, D), (ATT_WIDTH + LRU_WIDTH) ** -0.5),
        "e_conv_w": nrm((N_EVEN, LRU_CONV, LRU_WIDTH), LRU_CONV ** -0.5),
        "e_conv_b": nrm((N_EVEN, LRU_WIDTH), 0.02),
        "e_lru_wr": nrm((N_EVEN, 2, LRU_BLOCKS, LRU_BLOCK, LRU_BLOCK), LRU_BLOCK ** -0.5),
        "e_lru_br": nrm((N_EVEN, 2, LRU_WIDTH), 0.02),
        "e_lru_wi": nrm((N_EVEN, 2, LRU_BLOCKS, LRU_BLOCK, LRU_BLOCK), LRU_BLOCK ** -0.5),
        "e_lru_bi": nrm((N_EVEN, 2, LRU_WIDTH), 0.02),
        "e_lru_lam": lam,
        "e_rpb": nrm((N_EVEN, ATT_HEADS, 2 * WIN_ROWS_MAX - 1, 2 * WIN_COLS - 1), 0.1),
        "o_w_in": nrm((N_ODD, D, 3 * SC_WIDTH), D ** -0.5),
        "o_conv_w": nrm((N_ODD, SC_CONV, SC_WIDTH), SC_CONV ** -0.5),
        "o_w_out": nrm((N_ODD, SC_WIDTH, D), SC_WIDTH ** -0.5),
        "final_g": 1.0 + nrm((D,), 0.05),
    }


def reference(x, c, ctx, c_ctx, mod_w, mod_b, norm1_g, norm2_g, ffn_w_gu, ffn_w_down,
              e_w_in, e_w_out, e_conv_w, e_conv_b, e_lru_wr, e_lru_br, e_lru_wi, e_lru_bi,
              e_lru_lam, e_rpb, o_w_in, o_conv_w, o_w_out, final_g):
    h, hc = x, ctx
    for l in range(DEPTH):
        even = l % 2 == 0
        ctx_out = any(j % 2 == 0 for j in range(l + 1, DEPTH))
        sh1, sc1, g1, sh2, sc2, g2 = adaln(c, mod_w[l], mod_b[l])
        n_lat = modulate(rmsnorm(h, norm1_g[l]), sh1, sc1)
        if even or ctx_out:
            csh1, csc1, cg1, csh2, csc2, cg2 = adaln(c_ctx, mod_w[l], mod_b[l])
            n_ctx = modulate(rmsnorm(hc, norm1_g[l]), csh1, csc1)
        if even:
            e = l // 2
            y, yc = hybrid_mixer(n_lat, n_ctx, e_w_in[e], e_w_out[e], e_conv_w[e], e_conv_b[e],
                                 e_lru_wr[e], e_lru_br[e], e_lru_wi[e], e_lru_bi[e], e_lru_lam[e],
                                 e_rpb[e], ctx_out)
        else:
            o = l // 2
            y = short_conv_mixer(n_lat, o_w_in[o], o_conv_w[o], o_w_out[o])
            yc = short_conv_mixer(n_ctx, o_w_in[o], o_conv_w[o], o_w_out[o]) if ctx_out else None
        h = h + g1 * y
        h = h + g2 * swiglu(modulate(rmsnorm(h, norm2_g[l]), sh2, sc2), ffn_w_gu[l], ffn_w_down[l])
        if ctx_out:
            hc = hc + cg1 * yc
            hc = hc + cg2 * swiglu(modulate(rmsnorm(hc, norm2_g[l]), csh2, csc2), ffn_w_gu[l], ffn_w_down[l])
    return rmsnorm(h, final_g)
```

```python
import functools

import numpy as np
import jax
import jax.numpy as jnp
from jax import lax
from jax.experimental import pallas as pl
from jax.experimental.pallas import tpu as pltpu

F32 = jnp.float32
BF16 = jnp.bfloat16

D_MODEL = 1024
BATCH = 2
SEQ = 16384
TOKENS = BATCH * SEQ
GRID_W = 64
GRID_ROWS = SEQ // GRID_W
CTX_LEN = 256
HEAD_DIM = 64
ATT_WIDTH = 512
ATT_HEADS = ATT_WIDTH // HEAD_DIM
LRU_WIDTH = 512
LRU_BLOCKS = 8
LRU_BLOCK = LRU_WIDTH // LRU_BLOCKS
LRU_C = 8.0
WIN_ROWS = 8
WIN_COLS = 16
FFN_HIDDEN = 2816
EPS = 1e-6

V7X_SUBLANES = 8
V7X_MXU_DIM = 256
V7X_VMEM_BYTES = 64 * 1024 * 1024

Q_ROWS = 4
Q_BLK = Q_ROWS * GRID_W
N_QBLK = GRID_ROWS // Q_ROWS
WIN_KEYS = 3 * Q_BLK
MASK_VALUE = -1e30

TM = 512
LRU_TS = 256
FFN_CHUNKS = ((0, 512), (512, 512), (1024, 512), (1536, 512), (2048, 512), (2560, 256))
COND_ROWS = 8
CTX_ROW = BATCH

_NT_DIMS = (((1,), (1,)), ((), ()))


def _const_spec(shape):
    nd = len(shape)
    return pl.BlockSpec(shape, lambda *_: (0,) * nd, pipeline_mode=pl.Buffered(1))


def _params(vmem_bytes, ndims):
    return pltpu.CompilerParams(
        dimension_semantics=("arbitrary",) * ndims,
        vmem_limit_bytes=min(int(vmem_bytes), V7X_VMEM_BYTES - 6 * 1024 * 1024),
    )


def _norm_mod(x, gain, shift, scale):
    ms = jnp.mean(x * x, axis=-1, keepdims=True)
    y = x * lax.rsqrt(ms + EPS) * gain
    return y * (1.0 + scale) + shift


def _sigmoid(x):
    return 0.5 * jnp.tanh(0.5 * x) + 0.5


def _gelu_tanh(x):
    return 0.5 * x * (1.0 + jnp.tanh(0.7978845608028654 * (x + 0.044715 * (x * x * x))))


def _adaln_kernel(c_ref, w_ref, b_ref, o_ref):
    c = c_ref[...]
    s = c * _sigmoid(c)
    o_ref[...] = jnp.dot(s, w_ref[...], preferred_element_type=F32,
                         precision=lax.Precision.HIGHEST) + b_ref[...]


def _adaln(cond, mod_w, mod_b):
    depth = mod_w.shape[0]
    n_out = mod_w.shape[2]
    tn = 1536
    return pl.pallas_call(
        _adaln_kernel,
        grid=(depth, n_out // tn),
        in_specs=[
            pl.BlockSpec((COND_ROWS, D_MODEL), lambda l, j: (0, 0)),
            pl.BlockSpec((None, D_MODEL, tn), lambda l, j: (l, 0, j)),
            pl.BlockSpec((None, 1, tn), lambda l, j: (l, 0, j)),
        ],
        out_specs=pl.BlockSpec((None, COND_ROWS, tn), lambda l, j: (l, 0, j)),
        out_shape=jax.ShapeDtypeStruct((depth, COND_ROWS, n_out), F32),
        compiler_params=_params(3 * D_MODEL * tn * 4, 2),
        name="adaln",
    )(cond, mod_w, mod_b.reshape(depth, 1, n_out))


def _inproj0_kernel(x_ref, mod_ref, g_ref, wqT_ref, wvT_ref, wkxg_ref,
                    qT_ref, k_ref, vT_ref, xr_ref, gr_ref):
    n = _norm_mod(x_ref[...], g_ref[...], mod_ref[0:1, :], mod_ref[1:2, :]).astype(BF16)
    qT_ref[...] = lax.dot_general(wqT_ref[...], n, _NT_DIMS, preferred_element_type=F32).astype(BF16)
    vT_ref[...] = lax.dot_general(wvT_ref[...], n, _NT_DIMS, preferred_element_type=F32).astype(BF16)
    kxg = jnp.dot(n, wkxg_ref[...], preferred_element_type=F32)
    k_ref[...] = kxg[:, :ATT_WIDTH].astype(BF16)
    xr_ref[...] = kxg[:, ATT_WIDTH:ATT_WIDTH + LRU_WIDTH]
    gr_ref[...] = kxg[:, ATT_WIDTH + LRU_WIDTH:].astype(BF16)


def _inproj0(x2d, mod, gain, wqT, wvT, wkxg, tm, mod_row_of_tile):
    n_tok = x2d.shape[0]
    wide = ATT_WIDTH + 2 * LRU_WIDTH
    return pl.pallas_call(
        _inproj0_kernel,
        grid=(n_tok // tm,),
        in_specs=[
            pl.BlockSpec((tm, D_MODEL), lambda i: (i, 0)),
            pl.BlockSpec((None, 6, D_MODEL), lambda i: (mod_row_of_tile(i), 0, 0)),
            _const_spec((1, D_MODEL)),
            _const_spec((ATT_WIDTH, D_MODEL)),
            _const_spec((ATT_WIDTH, D_MODEL)),
            _const_spec((D_MODEL, wide)),
        ],
        out_specs=[
            pl.BlockSpec((ATT_WIDTH, tm), lambda i: (0, i)),
            pl.BlockSpec((tm, ATT_WIDTH), lambda i: (i, 0)),
            pl.BlockSpec((ATT_WIDTH, tm), lambda i: (0, i)),
            pl.BlockSpec((tm, LRU_WIDTH), lambda i: (i, 0)),
            pl.BlockSpec((tm, LRU_WIDTH), lambda i: (i, 0)),
        ],
        out_shape=[
            jax.ShapeDtypeStruct((ATT_WIDTH, n_tok), BF16),
            jax.ShapeDtypeStruct((n_tok, ATT_WIDTH), BF16),
            jax.ShapeDtypeStruct((ATT_WIDTH, n_tok), BF16),
            jax.ShapeDtypeStruct((n_tok, LRU_WIDTH), F32),
            jax.ShapeDtypeStruct((n_tok, LRU_WIDTH), BF16),
        ],
        compiler_params=_params(40 * 1024 * 1024, 1),
        name="inproj0",
    )(x2d, mod, gain, wqT, wvT, wkxg)


def _attention_kernel(qT_ref, kp_ref, kc_ref, kn_ref, vTp_ref, vTc_ref, vTn_ref,
                      kx_ref, vTx_ref, bias_ref, o_ref, oT_ref):
    k_refs = (kp_ref, kc_ref, kn_ref)
    vT_refs = (vTp_ref, vTc_ref, vTn_ref)
    slab = 2 * HEAD_DIM
    row = lax.broadcasted_iota(jnp.int32, (slab, Q_BLK), 0)
    for hp in range(ATT_HEADS // 2):
        lanes = slice(hp * slab, (hp + 1) * slab)
        q_slab = qT_ref[lanes, :]
        for e in range(2):
            h = 2 * hp + e
            keep = (row < HEAD_DIM) if e == 0 else (row >= HEAD_DIM)
            qh = jnp.where(keep, q_slab, jnp.zeros_like(q_slab))
            s = [jnp.dot(k_refs[p][:, lanes], qh, preferred_element_type=F32)
                 + bias_ref[h, p * Q_BLK:(p + 1) * Q_BLK, :] for p in range(3)]
            s.append(jnp.dot(kx_ref[:, lanes], qh, preferred_element_type=F32))
            m = jnp.max(s[0], axis=0, keepdims=True)
            for t in s[1:]:
                m = jnp.maximum(m, jnp.max(t, axis=0, keepdims=True))
            p_ = [jnp.exp(t - m) for t in s]
            l = jnp.sum(p_[0], axis=0, keepdims=True)
            for t in p_[1:]:
                l = l + jnp.sum(t, axis=0, keepdims=True)
            rows = slice(h * HEAD_DIM, (h + 1) * HEAD_DIM)
            acc = jnp.dot(vTx_ref[rows, :], p_[3].astype(BF16), preferred_element_type=F32)
            for p in range(3):
                acc = acc + jnp.dot(vT_refs[p][rows, :], p_[p].astype(BF16), preferred_element_type=F32)
            oT_ref[rows, :] = acc / l
    o_ref[...] = oT_ref[...].T.astype(BF16)


def _attention_bias(rpb):
    dr_idx = np.zeros((3, WIN_KEYS, Q_BLK), np.int32)
    dc_idx = np.zeros((3, WIN_KEYS, Q_BLK), np.int32)
    valid = np.zeros((3, WIN_KEYS, Q_BLK), bool)
    key = np.arange(WIN_KEYS)
    kp, krr, kcol = key // Q_BLK, (key % Q_BLK) // GRID_W, key % GRID_W
    qry = np.arange(Q_BLK)
    qrr, qcol = qry // GRID_W, qry % GRID_W
    col_start = np.clip(qcol - WIN_COLS // 2, 0, GRID_W - WIN_COLS)
    for pat, i in enumerate((0, 1, N_QBLK - 1)):
        blk = i - 1 + kp
        krow = Q_ROWS * blk + krr
        qrow = Q_ROWS * i + qrr
        row_start = np.clip(qrow - WIN_ROWS // 2, 0, GRID_ROWS - WIN_ROWS)
        ok = ((blk >= 0) & (blk < N_QBLK))[:, None]
        ok = ok & (krow[:, None] >= row_start[None, :]) & (krow[:, None] < row_start[None, :] + WIN_ROWS)
        ok = ok & (kcol[:, None] >= col_start[None, :]) & (kcol[:, None] < col_start[None, :] + WIN_COLS)
        dr = krow[:, None] - qrow[None, :] + (WIN_ROWS - 1)
        dc = kcol[:, None] - qcol[None, :] + (WIN_COLS - 1)
        valid[pat] = ok
        dr_idx[pat] = np.where(ok, dr, 0)
        dc_idx[pat] = np.where(ok, dc, 0)
    gathered = rpb[:, dr_idx, dc_idx]
    bias = jnp.where(valid[None], gathered, MASK_VALUE)
    return jnp.transpose(bias, (1, 0, 2, 3)).astype(F32)


def _attention(qT, k, vT, k_ctx, vT_ctx, bias):
    def blk(b, i):
        return b * N_QBLK + i

    def pattern(i):
        return jnp.where(i == 0, 0, jnp.where(i == N_QBLK - 1, 2, 1))

    k_spec = lambda f: pl.BlockSpec((Q_BLK, ATT_WIDTH), lambda b, i: (blk(b, f(i)), 0))
    vT_spec = lambda f: pl.BlockSpec((ATT_WIDTH, Q_BLK), lambda b, i: (0, blk(b, f(i))))
    prev = lambda i: jnp.maximum(i - 1, 0)
    cur = lambda i: i
    nxt = lambda i: jnp.minimum(i + 1, N_QBLK - 1)
    return pl.pallas_call(
        _attention_kernel,
        grid=(BATCH, N_QBLK),
        in_specs=[
            vT_spec(cur),
            k_spec(prev), k_spec(cur), k_spec(nxt),
            vT_spec(prev), vT_spec(cur), vT_spec(nxt),
            pl.BlockSpec((CTX_LEN, ATT_WIDTH), lambda b, i: (b, 0)),
            pl.BlockSpec((ATT_WIDTH, CTX_LEN), lambda b, i: (0, b)),
            pl.BlockSpec((None, ATT_HEADS, WIN_KEYS, Q_BLK), lambda b, i: (pattern(i), 0, 0, 0)),
        ],
        out_specs=pl.BlockSpec((Q_BLK, ATT_WIDTH), lambda b, i: (blk(b, i), 0)),
        out_shape=jax.ShapeDtypeStruct((TOKENS, ATT_WIDTH), BF16),
        scratch_shapes=[pltpu.VMEM((ATT_WIDTH, Q_BLK), F32)],
        compiler_params=_params(40 * 1024 * 1024, 2),
        name="attention",
    )(qT, k, k, k, vT, vT, vT, k_ctx, vT_ctx, bias)


def _lru_tile(x_full, conv_w, conv_b, wg_ref, b_r, b_i, sp, h_in, reverse):
    n_full = x_full.shape[0]
    n = n_full - 2 * V7X_SUBLANES
    body = slice(V7X_SUBLANES, V7X_SUBLANES + n)
    xc = (conv_w[0:1, :] * pltpu.roll(x_full, 2, 0)[body]
          + conv_w[1:2, :] * pltpu.roll(x_full, 1, 0)[body]
          + conv_w[2:3, :] * x_full[body]
          + conv_w[3:4, :] * pltpu.roll(x_full, n_full - 1, 0)[body]) + conv_b
    half = LRU_WIDTH // 2
    xb = xc.astype(BF16)
    g0 = jnp.dot(xb[:, :half], wg_ref[0], preferred_element_type=F32)
    g1 = jnp.dot(xb[:, half:], wg_ref[1], preferred_element_type=F32)
    r = _sigmoid(jnp.concatenate([g0[:, :half], g1[:, :half]], axis=1) + b_r)
    i = _sigmoid(jnp.concatenate([g0[:, half:], g1[:, half:]], axis=1) + b_i)
    log_a = (-LRU_C) * r * sp
    a = jnp.exp(log_a)
    th = jnp.tanh(log_a)
    u = jnp.sqrt((-2.0 * th) / (1.0 - th)) * (i * xc)
    row = lax.broadcasted_iota(jnp.int32, (n, LRU_WIDTH), 0)
    d = 1
    while d < n:
        if reverse:
            take = row < n - d
            shift = n - d
        else:
            take = row >= d
            shift = d
        a_s = jnp.where(take, pltpu.roll(a, shift, 0), 1.0)
        u_s = jnp.where(take, pltpu.roll(u, shift, 0), 0.0)
        u = a * u_s + u
        a = a * a_s
        d *= 2
    return u + a * h_in


def _lru_kernel(x_ref, xprev_ref, xnext_ref, xctx_ref, cw_ref, cb_ref, wg_ref, br_ref, bi_ref, lam_ref,
                o_ref, carry_ref, *, reverse, n_tiles):
    j = pl.program_id(1)
    t = (n_tiles - 1 - j) if reverse else j
    z = -lam_ref[...]
    sp = jnp.maximum(z, 0.0) + jnp.log1p(jnp.exp(-jnp.abs(z)))
    conv_w = cw_ref[...]
    conv_b = cb_ref[...]
    b_r = br_ref[...]
    b_i = bi_ref[...]
    zeros8 = jnp.zeros((V7X_SUBLANES, LRU_WIDTH), F32)

    @pl.when(j == 0)
    def _():
        xc_full = jnp.concatenate([zeros8, xctx_ref[...], zeros8], axis=0)
        h_c = _lru_tile(xc_full, conv_w, conv_b, wg_ref, b_r, b_i, sp,
                        jnp.zeros((1, LRU_WIDTH), F32), reverse)
        carry_ref[0:1, :] = h_c[0:1, :] if reverse else h_c[CTX_LEN - 1:CTX_LEN, :]

    xprev = jnp.where(t == 0, zeros8, xprev_ref[...])
    xnext = jnp.where(t == n_tiles - 1, zeros8, xnext_ref[...])
    x_full = jnp.concatenate([xprev, x_ref[...], xnext], axis=0)
    h = _lru_tile(x_full, conv_w, conv_b, wg_ref, b_r, b_i, sp, carry_ref[0:1, :], reverse)
    o_ref[...] = h
    carry_ref[0:1, :] = h[0:1, :] if reverse else h[LRU_TS - 1:LRU_TS, :]


def _lru(xr, xr_ctx, conv_w, conv_b, wg, b_r, b_i, lam, reverse):
    n_tiles = SEQ // LRU_TS
    sub = LRU_TS // V7X_SUBLANES
    last8 = TOKENS // V7X_SUBLANES - 1

    def tile(j):
        return (n_tiles - 1 - j) if reverse else j

    vec = lambda: _const_spec((1, LRU_WIDTH))
    return pl.pallas_call(
        functools.partial(_lru_kernel, reverse=reverse, n_tiles=n_tiles),
        grid=(BATCH, n_tiles),
        in_specs=[
            pl.BlockSpec((LRU_TS, LRU_WIDTH), lambda b, j: (b * n_tiles + tile(j), 0)),
            pl.BlockSpec((V7X_SUBLANES, LRU_WIDTH),
                         lambda b, j: (jnp.maximum((b * n_tiles + tile(j)) * sub - 1, 0), 0)),
            pl.BlockSpec((V7X_SUBLANES, LRU_WIDTH),
                         lambda b, j: (jnp.minimum((b * n_tiles + tile(j) + 1) * sub, last8), 0)),
            pl.BlockSpec((CTX_LEN, LRU_WIDTH), lambda b, j: (b, 0)),
            _const_spec((4, LRU_WIDTH)),
            vec(),
            _const_spec((2, LRU_WIDTH // 2, LRU_WIDTH)),
            vec(), vec(), vec(),
        ],
        out_specs=pl.BlockSpec((LRU_TS, LRU_WIDTH), lambda b, j: (b * n_tiles + tile(j), 0)),
        out_shape=jax.ShapeDtypeStruct((TOKENS, LRU_WIDTH), F32),
        scratch_shapes=[pltpu.VMEM((V7X_SUBLANES, LRU_WIDTH), F32)],
        compiler_params=_params(32 * 1024 * 1024, 2),
        name="lru_bwd" if reverse else "lru_fwd",
    )(xr, xr, xr, xr_ctx, conv_w, conv_b, wg, b_r, b_i, lam)


def _lru_gate_weights(w_r, w_i):
    def half_diag(w, hf):
        out = jnp.zeros((LRU_WIDTH // 2, LRU_WIDTH // 2), F32)
        for g in range(LRU_BLOCKS // 2):
            lo = g * LRU_BLOCK
            out = out.at[lo:lo + LRU_BLOCK, lo:lo + LRU_BLOCK].set(w[hf * (LRU_BLOCKS // 2) + g])
        return out
    halves = [jnp.concatenate([half_diag(w_r, hf), half_diag(w_i, hf)], axis=1) for hf in range(2)]
    return jnp.stack(halves).astype(BF16)


def _ffn_tail(h, mod_ref, g2_ref, wgu_ref, wdown_ref):
    n2 = _norm_mod(h, g2_ref[...], mod_ref[3:4, :], mod_ref[4:5, :]).astype(BF16)
    acc = jnp.zeros(h.shape, F32)
    for c0, width in FFN_CHUNKS:
        gt = jnp.dot(n2, wgu_ref[:, c0:c0 + width], preferred_element_type=F32)
        up = jnp.dot(n2, wgu_ref[:, FFN_HIDDEN + c0:FFN_HIDDEN + c0 + width], preferred_element_type=F32)
        act = (gt * _sigmoid(gt) * up).astype(BF16)
        acc = acc + jnp.dot(act, wdown_ref[c0:c0 + width, :], preferred_element_type=F32)
    return h + mod_ref[5:6, :] * acc


def _post0_kernel(x_ref, att_ref, hf_ref, hb_ref, gr_ref, mod_ref, g2_ref, wout_ref, wgu_ref, wdown_ref, o_ref):
    rec = ((hf_ref[...] + hb_ref[...]) * _gelu_tanh(gr_ref[...].astype(F32))).astype(BF16)
    y = (jnp.dot(att_ref[...], wout_ref[:ATT_WIDTH, :], preferred_element_type=F32)
         + jnp.dot(rec, wout_ref[ATT_WIDTH:, :], preferred_element_type=F32))
    h = x_ref[...] + mod_ref[2:3, :] * y
    o_ref[...] = _ffn_tail(h, mod_ref, g2_ref, wgu_ref, wdown_ref)


def _post0(x2d, att, h_fwd, h_bwd, gr, mod, g2, w_out, w_gu, w_down):
    tiles_per_sample = SEQ // TM
    tok = lambda w: pl.BlockSpec((TM, w), lambda i: (i, 0))
    return pl.pallas_call(
        _post0_kernel,
        grid=(TOKENS // TM,),
        in_specs=[
            tok(D_MODEL), tok(ATT_WIDTH), tok(LRU_WIDTH), tok(LRU_WIDTH), tok(LRU_WIDTH),
            pl.BlockSpec((None, 6, D_MODEL), lambda i: (i // tiles_per_sample, 0, 0)),
            _const_spec((1, D_MODEL)),
            _const_spec((D_MODEL, D_MODEL)),
            _const_spec((D_MODEL, 2 * FFN_HIDDEN)),
            _const_spec((FFN_HIDDEN, D_MODEL)),
        ],
        out_specs=tok(D_MODEL),
        out_shape=jax.ShapeDtypeStruct((TOKENS, D_MODEL), F32),
        compiler_params=_params(56 * 1024 * 1024, 1),
        name="post0",
    )(x2d, att, h_fwd, h_bwd, gr, mod, g2, w_out, w_gu, w_down)


def _layer1_kernel(h_ref, hprev_ref, hnext_ref, mod_ref, g1_ref, win_ref, cw_ref, wout_ref,
                   g2_ref, wgu_ref, wdown_ref, gf_ref, o_ref, *, tiles_per_sample):
    tt = pl.program_id(0) % tiles_per_sample
    h = h_ref[...]
    h_full = jnp.concatenate([hprev_ref[...], h, hnext_ref[...]], axis=0)
    n_full = h_full.shape[0]
    body = slice(V7X_SUBLANES, V7X_SUBLANES + TM)
    n = _norm_mod(h_full, g1_ref[...], mod_ref[0:1, :], mod_ref[1:2, :]).astype(BF16)
    cx = jnp.dot(n, win_ref[:, D_MODEL:], preferred_element_type=F32)
    prod = cx[:, :D_MODEL] * cx[:, D_MODEL:]
    row = lax.broadcasted_iota(jnp.int32, (n_full, D_MODEL), 0)
    outside = ((row < V7X_SUBLANES) & (tt == 0)) | ((row >= V7X_SUBLANES + TM) & (tt == tiles_per_sample - 1))
    prod = jnp.where(outside, 0.0, prod)
    conv = (cw_ref[0:1, :] * pltpu.roll(prod, 1, 0)[body]
            + cw_ref[1:2, :] * prod[body]
            + cw_ref[2:3, :] * pltpu.roll(prod, n_full - 1, 0)[body])
    bg = jnp.dot(n[body], win_ref[:, :D_MODEL], preferred_element_type=F32)
    y = jnp.dot((bg * conv).astype(BF16), wout_ref[...], preferred_element_type=F32)
    h1 = h + mod_ref[2:3, :] * y
    h2 = _ffn_tail(h1, mod_ref, g2_ref, wgu_ref, wdown_ref)
    ms = jnp.mean(h2 * h2, axis=-1, keepdims=True)
    o_ref[...] = h2 * lax.rsqrt(ms + EPS) * gf_ref[...]


def _layer1(h, mod, g1, w_in, conv_w, w_out, g2, w_gu, w_down, g_final):
    tiles_per_sample = SEQ // TM
    sub = TM // V7X_SUBLANES
    last8 = TOKENS // V7X_SUBLANES - 1
    return pl.pallas_call(
        functools.partial(_layer1_kernel, tiles_per_sample=tiles_per_sample),
        grid=(TOKENS // TM,),
        in_specs=[
            pl.BlockSpec((TM, D_MODEL), lambda i: (i, 0)),
            pl.BlockSpec((V7X_SUBLANES, D_MODEL), lambda i: (jnp.maximum(i * sub - 1, 0), 0)),
            pl.BlockSpec((V7X_SUBLANES, D_MODEL), lambda i: (jnp.minimum((i + 1) * sub, last8), 0)),
            pl.BlockSpec((None, 6, D_MODEL), lambda i: (i // tiles_per_sample, 0, 0)),
            _const_spec((1, D_MODEL)),
            _const_spec((D_MODEL, 3 * D_MODEL)),
            _const_spec((3, D_MODEL)),
            _const_spec((D_MODEL, D_MODEL)),
            _const_spec((1, D_MODEL)),
            _const_spec((D_MODEL, 2 * FFN_HIDDEN)),
            _const_spec((FFN_HIDDEN, D_MODEL)),
            _const_spec((1, D_MODEL)),
        ],
        out_specs=pl.BlockSpec((TM, D_MODEL), lambda i: (i, 0)),
        out_shape=jax.ShapeDtypeStruct((TOKENS, D_MODEL), F32),
        compiler_params=_params(58 * 1024 * 1024, 1),
        name="layer1",
    )(h, h, h, mod, g1, w_in, conv_w, w_out, g2, w_gu, w_down, g_final)


def kernel(x, c, ctx, c_ctx, mod_w, mod_b, norm1_g, norm2_g, ffn_w_gu, ffn_w_down, e_w_in, e_w_out, e_conv_w, e_conv_b, e_lru_wr, e_lru_br, e_lru_wi, e_lru_bi, e_lru_lam, e_rpb, o_w_in, o_conv_w, o_w_out, final_g):
    depth = mod_w.shape[0]
    x2d = x.reshape(TOKENS, D_MODEL)
    ctx2d = ctx.reshape(BATCH * CTX_LEN, D_MODEL)

    cond = jnp.zeros((COND_ROWS, D_MODEL), F32).at[:BATCH].set(c).at[CTX_ROW].set(c_ctx)
    mods = _adaln(cond, mod_w, mod_b).reshape(depth, COND_ROWS, 6, D_MODEL)

    w_in = e_w_in[0]
    scale = HEAD_DIM ** -0.5
    wqT = (w_in[:, :ATT_WIDTH].T * scale).astype(BF16)
    wvT = w_in[:, 2 * ATT_WIDTH:3 * ATT_WIDTH].T.astype(BF16)
    wkxg = jnp.concatenate([w_in[:, ATT_WIDTH:2 * ATT_WIDTH], w_in[:, 3 * ATT_WIDTH:]], axis=1).astype(BF16)
    g1 = norm1_g[0].reshape(1, D_MODEL)

    tiles_per_sample = SEQ // TM
    qT, k, vT, xr, gr = _inproj0(x2d, mods[0], g1, wqT, wvT, wkxg, TM, lambda i: i // tiles_per_sample)
    _, k_ctx, vT_ctx, xr_ctx, _ = _inproj0(ctx2d, mods[0], g1, wqT, wvT, wkxg, CTX_LEN, lambda i: CTX_ROW)

    att = _attention(qT, k, vT, k_ctx, vT_ctx, _attention_bias(e_rpb[0]))

    conv_w = e_conv_w[0]
    conv_b = e_conv_b[0].reshape(1, LRU_WIDTH)
    lat = []
    for d, reverse in enumerate((False, True)):
        wg = _lru_gate_weights(e_lru_wr[0, d], e_lru_wi[0, d])
        lat.append(_lru(xr, xr_ctx, conv_w, conv_b, wg,
                        e_lru_br[0, d].reshape(1, LRU_WIDTH), e_lru_bi[0, d].reshape(1, LRU_WIDTH),
                        e_lru_lam[0, d].reshape(1, LRU_WIDTH), reverse))

    h = _post0(x2d, att, lat[0], lat[1], gr, mods[0], norm2_g[0].reshape(1, D_MODEL),
               e_w_out[0].astype(BF16), ffn_w_gu[0].astype(BF16), ffn_w_down[0].astype(BF16))

    out = _layer1(h, mods[1], norm1_g[1].reshape(1, D_MODEL), o_w_in[0].astype(BF16), o_conv_w[0],
                  o_w_out[0].astype(BF16), norm2_g[1].reshape(1, D_MODEL),
                  ffn_w_gu[1].astype(BF16), ffn_w_down[1].astype(BF16), final_g.reshape(1, D_MODEL))
    return out.reshape(BATCH, SEQ, D_MODEL)
```

```python
import functools

import numpy as np
import jax
import jax.numpy as jnp
from jax import lax
from jax.experimental import pallas as pl
from jax.experimental.pallas import tpu as pltpu

F32 = jnp.float32
BF16 = jnp.bfloat16

D_MODEL = 1024
BATCH = 2
SEQ = 16384
TOKENS = BATCH * SEQ
GRID_W = 64
GRID_ROWS = SEQ // GRID_W
CTX_LEN = 256
HEAD_DIM = 64
ATT_WIDTH = 512
ATT_HEADS = ATT_WIDTH // HEAD_DIM
LRU_WIDTH = 512
LRU_BLOCKS = 8
LRU_BLOCK = LRU_WIDTH // LRU_BLOCKS
LRU_C = 8.0
WIN_ROWS = 8
WIN_COLS = 16
FFN_HIDDEN = 2816
EPS = 1e-6

V7X_SUBLANES = 8
V7X_LANES = 128
V7X_MXU_DIM = 256
V7X_VMEM_BYTES = 64 * 1024 * 1024

Q_ROWS = 4
Q_BLK = Q_ROWS * GRID_W
N_QBLK = GRID_ROWS // Q_ROWS
WIN_KEYS = 3 * Q_BLK
MASK_VALUE = -1e30

TM = 512
LRU_TS = 1024
LRU_SLABS = LRU_WIDTH // V7X_LANES
LRU_CHUNK = 256
SCAN_SEG = 4
SCAN_GROUP = V7X_SUBLANES * SCAN_SEG
SCAN_TOP = 16
LOG2E = 1.4426950408889634
F32_TINY = float(np.finfo(np.float32).tiny)
FFN_CHUNKS = ((0, 512), (512, 512), (1024, 512), (1536, 512), (2048, 512), (2560, 256))
COND_ROWS = 8
CTX_ROW = BATCH

_NT_DIMS = (((1,), (1,)), ((), ()))


def _const_spec(shape):
    nd = len(shape)
    return pl.BlockSpec(shape, lambda *_: (0,) * nd, pipeline_mode=pl.Buffered(1))


def _params(vmem_bytes, ndims):
    return pltpu.CompilerParams(
        dimension_semantics=("arbitrary",) * ndims,
        vmem_limit_bytes=min(int(vmem_bytes), V7X_VMEM_BYTES - 6 * 1024 * 1024),
    )


def _norm_mod(x, gain, shift, scale):
    ms = jnp.mean(x * x, axis=-1, keepdims=True)
    y = x * lax.rsqrt(ms + EPS) * gain
    return y * (1.0 + scale) + shift


def _sigmoid(x):
    return 0.5 * jnp.tanh(0.5 * x) + 0.5


def _gelu_tanh(x):
    return 0.5 * x * (1.0 + jnp.tanh(0.7978845608028654 * (x + 0.044715 * (x * x * x))))


def _adaln_kernel(c_ref, w_ref, b_ref, o_ref):
    c = c_ref[...]
    s = c * _sigmoid(c)
    o_ref[...] = jnp.dot(s, w_ref[...], preferred_element_type=F32,
                         precision=lax.Precision.HIGHEST) + b_ref[...]


def _adaln(cond, mod_w, mod_b):
    depth = mod_w.shape[0]
    n_out = mod_w.shape[2]
    tn = 1536
    return pl.pallas_call(
        _adaln_kernel,
        grid=(depth, n_out // tn),
        in_specs=[
            pl.BlockSpec((COND_ROWS, D_MODEL), lambda l, j: (0, 0)),
            pl.BlockSpec((None, D_MODEL, tn), lambda l, j: (l, 0, j)),
            pl.BlockSpec((None, 1, tn), lambda l, j: (l, 0, j)),
        ],
        out_specs=pl.BlockSpec((None, COND_ROWS, tn), lambda l, j: (l, 0, j)),
        out_shape=jax.ShapeDtypeStruct((depth, COND_ROWS, n_out), F32),
        compiler_params=_params(3 * D_MODEL * tn * 4, 2),
        name="adaln",
    )(cond, mod_w, mod_b.reshape(depth, 1, n_out))


def _inproj0_kernel(x_ref, mod_ref, g_ref, wqT_ref, wvT_ref, wkxg_ref,
                    qT_ref, k_ref, vT_ref, xr_ref, gr_ref):
    n = _norm_mod(x_ref[...], g_ref[...], mod_ref[0:1, :], mod_ref[1:2, :]).astype(BF16)
    qT_ref[...] = lax.dot_general(wqT_ref[...], n, _NT_DIMS, preferred_element_type=F32).astype(BF16)
    vT_ref[...] = lax.dot_general(wvT_ref[...], n, _NT_DIMS, preferred_element_type=F32).astype(BF16)
    kxg = jnp.dot(n, wkxg_ref[...], preferred_element_type=F32)
    k_ref[...] = kxg[:, :ATT_WIDTH].astype(BF16)
    for s in range(LRU_SLABS):
        lo = ATT_WIDTH + s * V7X_LANES
        xr_ref[s] = kxg[:, lo:lo + V7X_LANES]
    gr_ref[...] = kxg[:, ATT_WIDTH + LRU_WIDTH:].astype(BF16)


def _inproj0(x2d, mod, gain, wqT, wvT, wkxg, tm, mod_row_of_tile):
    n_tok = x2d.shape[0]
    wide = ATT_WIDTH + 2 * LRU_WIDTH
    return pl.pallas_call(
        _inproj0_kernel,
        grid=(n_tok // tm,),
        in_specs=[
            pl.BlockSpec((tm, D_MODEL), lambda i: (i, 0)),
            pl.BlockSpec((None, 6, D_MODEL), lambda i: (mod_row_of_tile(i), 0, 0)),
            _const_spec((1, D_MODEL)),
            _const_spec((ATT_WIDTH, D_MODEL)),
            _const_spec((ATT_WIDTH, D_MODEL)),
            _const_spec((D_MODEL, wide)),
        ],
        out_specs=[
            pl.BlockSpec((ATT_WIDTH, tm), lambda i: (0, i)),
            pl.BlockSpec((tm, ATT_WIDTH), lambda i: (i, 0)),
            pl.BlockSpec((ATT_WIDTH, tm), lambda i: (0, i)),
            pl.BlockSpec((LRU_SLABS, tm, V7X_LANES), lambda i: (0, i, 0)),
            pl.BlockSpec((tm, LRU_WIDTH), lambda i: (i, 0)),
        ],
        out_shape=[
            jax.ShapeDtypeStruct((ATT_WIDTH, n_tok), BF16),
            jax.ShapeDtypeStruct((n_tok, ATT_WIDTH), BF16),
            jax.ShapeDtypeStruct((ATT_WIDTH, n_tok), BF16),
            jax.ShapeDtypeStruct((LRU_SLABS, n_tok, V7X_LANES), F32),
            jax.ShapeDtypeStruct((n_tok, LRU_WIDTH), BF16),
        ],
        compiler_params=_params(40 * 1024 * 1024, 1),
        name="inproj0",
    )(x2d, mod, gain, wqT, wvT, wkxg)


def _attention_kernel(qT_ref, kp_ref, kc_ref, kn_ref, vTp_ref, vTc_ref, vTn_ref,
                      kx_ref, vTx_ref, bias_ref, o_ref, oT_ref):
    k_refs = (kp_ref, kc_ref, kn_ref)
    vT_refs = (vTp_ref, vTc_ref, vTn_ref)
    slab = 2 * HEAD_DIM
    row = lax.broadcasted_iota(jnp.int32, (slab, Q_BLK), 0)
    for hp in range(ATT_HEADS // 2):
        lanes = slice(hp * slab, (hp + 1) * slab)
        q_slab = qT_ref[lanes, :]
        for e in range(2):
            h = 2 * hp + e
            keep = (row < HEAD_DIM) if e == 0 else (row >= HEAD_DIM)
            qh = jnp.where(keep, q_slab, jnp.zeros_like(q_slab))
            s = [jnp.dot(k_refs[p][:, lanes], qh, preferred_element_type=F32)
                 + bias_ref[h, p * Q_BLK:(p + 1) * Q_BLK, :] for p in range(3)]
            s.append(jnp.dot(kx_ref[:, lanes], qh, preferred_element_type=F32))
            m = jnp.max(s[0], axis=0, keepdims=True)
            for t in s[1:]:
                m = jnp.maximum(m, jnp.max(t, axis=0, keepdims=True))
            p_ = [jnp.exp2(t - m) for t in s]
            l = jnp.sum(p_[0], axis=0, keepdims=True)
            for t in p_[1:]:
                l = l + jnp.sum(t, axis=0, keepdims=True)
            rows = slice(h * HEAD_DIM, (h + 1) * HEAD_DIM)
            acc = jnp.dot(vTx_ref[rows, :], p_[3].astype(BF16), preferred_element_type=F32)
            for p in range(3):
                acc = acc + jnp.dot(vT_refs[p][rows, :], p_[p].astype(BF16), preferred_element_type=F32)
            oT_ref[rows, :] = acc / l
    o_ref[...] = oT_ref[...].T.astype(BF16)


def _attention_bias(rpb):
    n_dr, n_dc = 2 * WIN_ROWS - 1, 2 * WIN_COLS - 1
    kcol = np.arange(GRID_W)[:, None]
    qcol = np.arange(GRID_W)[None, :]
    col_start = np.clip(qcol - WIN_COLS // 2, 0, GRID_W - WIN_COLS)
    col_ok = (kcol >= col_start) & (kcol < col_start + WIN_COLS)
    dc = kcol - qcol + (WIN_COLS - 1)
    col_sel = ((dc[None] == np.arange(n_dc)[:, None, None]) & col_ok[None]).astype(np.float32)
    band = jnp.einsum("hrd,dkq->hrkq", rpb.astype(F32) * LOG2E, col_sel, precision=lax.Precision.HIGHEST)
    band = jnp.where(col_ok[None, None], band, MASK_VALUE)
    band = jnp.concatenate([band, jnp.full((ATT_HEADS, 1, GRID_W, GRID_W), MASK_VALUE, F32)], axis=1)
    n_krow = 3 * Q_ROWS
    row_sel = np.zeros((3, n_krow, Q_ROWS, n_dr + 1), np.float32)
    for pat, i in enumerate((0, 1, N_QBLK - 1)):
        for kr in range(n_krow):
            blk = i - 1 + kr // Q_ROWS
            krow = Q_ROWS * blk + kr % Q_ROWS
            for qr in range(Q_ROWS):
                qrow = Q_ROWS * i + qr
                row_start = min(max(qrow - WIN_ROWS // 2, 0), GRID_ROWS - WIN_ROWS)
                ok = 0 <= blk < N_QBLK and row_start <= krow < row_start + WIN_ROWS
                row_sel[pat, kr, qr, (krow - qrow + WIN_ROWS - 1) if ok else n_dr] = 1.0
    bias = jnp.einsum("pkqr,hrcd->phkcqd", row_sel, band, precision=lax.Precision.HIGHEST)
    return bias.reshape(3, ATT_HEADS, WIN_KEYS, Q_BLK)


def _attention(qT, k, vT, k_ctx, vT_ctx, bias):
    def blk(b, i):
        return b * N_QBLK + i

    def pattern(i):
        return jnp.where(i == 0, 0, jnp.where(i == N_QBLK - 1, 2, 1))

    k_spec = lambda f: pl.BlockSpec((Q_BLK, ATT_WIDTH), lambda b, i: (blk(b, f(i)), 0))
    vT_spec = lambda f: pl.BlockSpec((ATT_WIDTH, Q_BLK), lambda b, i: (0, blk(b, f(i))))
    prev = lambda i: jnp.maximum(i - 1, 0)
    cur = lambda i: i
    nxt = lambda i: jnp.minimum(i + 1, N_QBLK - 1)
    return pl.pallas_call(
        _attention_kernel,
        grid=(BATCH, N_QBLK),
        in_specs=[
            vT_spec(cur),
            k_spec(prev), k_spec(cur), k_spec(nxt),
            vT_spec(prev), vT_spec(cur), vT_spec(nxt),
            pl.BlockSpec((CTX_LEN, ATT_WIDTH), lambda b, i: (b, 0)),
            pl.BlockSpec((ATT_WIDTH, CTX_LEN), lambda b, i: (0, b)),
            pl.BlockSpec((None, ATT_HEADS, WIN_KEYS, Q_BLK), lambda b, i: (pattern(i), 0, 0, 0)),
        ],
        out_specs=pl.BlockSpec((Q_BLK, ATT_WIDTH), lambda b, i: (blk(b, i), 0)),
        out_shape=jax.ShapeDtypeStruct((TOKENS, ATT_WIDTH), BF16),
        scratch_shapes=[pltpu.VMEM((ATT_WIDTH, Q_BLK), F32)],
        compiler_params=_params(40 * 1024 * 1024, 2),
        name="attention",
    )(qT, k, k, k, vT, vT, vT, k_ctx, vT_ctx, bias)


def _scan_levels(n):
    rows = [n]
    while rows[-1] > SCAN_TOP:
        assert rows[-1] % SCAN_GROUP == 0
        rows.append(rows[-1] // SCAN_SEG)
    return rows


def _lru_gates(xbuf, n, a_ref, u_ref, conv_w, conv_b, wg_ref, b_r_half, b_i_half, c_half):
    half = LRU_WIDTH // 2
    for c0 in range(0, n, LRU_CHUNK):
        xc = []
        for s in range(LRU_SLABS):
            ln = slice(s * V7X_LANES, (s + 1) * V7X_LANES)
            acc = conv_b[:, ln]
            for k in range(4):
                lo = c0 + V7X_SUBLANES - 2 + k
                acc = acc + conv_w[k:k + 1, ln] * xbuf[s, lo:lo + LRU_CHUNK, :]
            xc.append(acc)
        g = [jnp.dot(jnp.concatenate(xc[2 * hf:2 * hf + 2], axis=1).astype(BF16), wg_ref[hf],
                     preferred_element_type=F32) for hf in range(2)]
        for s in range(LRU_SLABS):
            hf, q = divmod(s, 2)
            ln = slice(s * V7X_LANES, (s + 1) * V7X_LANES)
            zr = g[hf][:, q * V7X_LANES:(q + 1) * V7X_LANES] + b_r_half[:, ln]
            zi = g[hf][:, half + q * V7X_LANES:half + (q + 1) * V7X_LANES] + b_i_half[:, ln]
            log_a = c_half[:, ln] * jnp.tanh(zr) + c_half[:, ln]
            th = jnp.tanh(log_a)
            m2 = (-2.0 * th) / (1.0 - th)
            mult = m2 * lax.rsqrt(jnp.maximum(m2, F32_TINY))
            ix = (0.5 * jnp.tanh(zi) + 0.5) * xc[s]
            a_ref[s, c0:c0 + LRU_CHUNK, :] = jnp.exp(log_a)
            u_ref[s, c0:c0 + LRU_CHUNK, :] = mult * ix


def _lru_scan(lvl_a, lvl_u, lvl_t, out_ref, n, carry_in, reverse):
    rows = _scan_levels(n)
    top = len(rows) - 1
    order = tuple(reversed(range(SCAN_SEG))) if reverse else tuple(range(SCAN_SEG))
    t_off = (V7X_SUBLANES - 1) if reverse else (V7X_SUBLANES + 1)

    def seg_loads(ref, s, base):
        return [ref[s, pl.ds(base + j, V7X_SUBLANES, stride=SCAN_SEG), :] for j in range(SCAN_SEG)]

    for k in range(top):
        def up(g, carry, k=k):
            base = g * SCAN_GROUP
            dst = pl.multiple_of(g * V7X_SUBLANES, V7X_SUBLANES)
            for s in range(LRU_SLABS):
                a = seg_loads(lvl_a[k], s, base)
                u = seg_loads(lvl_u[k], s, base)
                p, h = a[order[0]], u[order[0]]
                for j in order[1:]:
                    h = a[j] * h + u[j]
                    p = a[j] * p
                lvl_a[k + 1][s, pl.ds(dst, V7X_SUBLANES), :] = p
                lvl_u[k + 1][s, pl.ds(dst, V7X_SUBLANES), :] = h
            return carry
        lax.fori_loop(0, rows[k] // SCAN_GROUP, up, 0, unroll=2)

    carry_out = []
    for s in range(LRU_SLABS):
        h = carry_in[s]
        for r in (reversed(range(rows[top])) if reverse else range(rows[top])):
            h = lvl_a[top][s, r:r + 1, :] * h + lvl_u[top][s, r:r + 1, :]
            if out_ref is not None:
                lvl_t[top][s, t_off + r:t_off + r + 1, :] = h
        carry_out.append(h)
    if out_ref is None:
        return carry_out

    for k in reversed(range(top)):
        c_row = (rows[k + 1] + V7X_SUBLANES - 1) if reverse else V7X_SUBLANES
        for s in range(LRU_SLABS):
            lvl_t[k + 1][s, c_row:c_row + 1, :] = carry_in[s]

        def down(g, carry, k=k):
            base = g * SCAN_GROUP
            src = pl.multiple_of((g + 1) * V7X_SUBLANES, V7X_SUBLANES)
            for s in range(LRU_SLABS):
                a = seg_loads(lvl_a[k], s, base)
                u = seg_loads(lvl_u[k], s, base)
                h = lvl_t[k + 1][s, pl.ds(src, V7X_SUBLANES), :]
                for j in order:
                    h = a[j] * h + u[j]
                    if k == 0:
                        out_ref[s, pl.ds(base + j, V7X_SUBLANES, stride=SCAN_SEG), :] = h
                    else:
                        lvl_t[k][s, pl.ds(t_off + base + j, V7X_SUBLANES, stride=SCAN_SEG), :] = h
            return carry
        lax.fori_loop(0, rows[k] // SCAN_GROUP, down, 0, unroll=2)
    return carry_out


def _lru_kernel(x_ref, xprev_ref, xnext_ref, xctx_ref, cw_ref, cb_ref, wg_ref, br_ref, bi_ref, lam_ref,
                o_ref, carry_ref, xbuf, *lvl, reverse, n_tiles):
    n_lvl = len(_scan_levels(LRU_TS))
    lvl_a, lvl_u, lvl_t = lvl[:n_lvl], lvl[n_lvl:2 * n_lvl], (None,) + tuple(lvl[2 * n_lvl:])
    j = pl.program_id(1)
    t = (n_tiles - 1 - j) if reverse else j
    z = -lam_ref[...]
    sp = jnp.maximum(z, 0.0) + jnp.log1p(jnp.exp(-jnp.abs(z)))
    c_half = (-0.5 * LRU_C) * sp
    conv_w = cw_ref[...]
    conv_b = cb_ref[...]
    gate_args = (conv_w, conv_b, wg_ref, br_ref[...], bi_ref[...], c_half)
    zeros8 = jnp.zeros((V7X_SUBLANES, V7X_LANES), F32)
    body = V7X_SUBLANES

    @pl.when(j == 0)
    def _():
        for s in range(LRU_SLABS):
            xbuf[s, 0:body, :] = zeros8
            xbuf[s, body:body + CTX_LEN, :] = xctx_ref[s]
            xbuf[s, body + CTX_LEN:2 * body + CTX_LEN, :] = zeros8
        _lru_gates(xbuf, CTX_LEN, lvl_a[0], lvl_u[0], *gate_args)
        h0 = _lru_scan(lvl_a, lvl_u, lvl_t, None, CTX_LEN,
                       [jnp.zeros((1, V7X_LANES), F32)] * LRU_SLABS, reverse)
        for s in range(LRU_SLABS):
            carry_ref[s, 0:1, :] = h0[s]

    for s in range(LRU_SLABS):
        xbuf[s, 0:body, :] = jnp.where(t == 0, zeros8, xprev_ref[s])
        xbuf[s, body:body + LRU_TS, :] = x_ref[s]
        xbuf[s, body + LRU_TS:2 * body + LRU_TS, :] = jnp.where(t == n_tiles - 1, zeros8, xnext_ref[s])
    _lru_gates(xbuf, LRU_TS, lvl_a[0], lvl_u[0], *gate_args)
    h_out = _lru_scan(lvl_a, lvl_u, lvl_t, o_ref, LRU_TS,
                      [carry_ref[s, 0:1, :] for s in range(LRU_SLABS)], reverse)
    for s in range(LRU_SLABS):
        carry_ref[s, 0:1, :] = h_out[s]


def _lru(xr, xr_ctx, conv_w, conv_b, wg_half, b_r_half, b_i_half, lam, reverse):
    n_tiles = SEQ // LRU_TS
    sub = LRU_TS // V7X_SUBLANES
    last8 = TOKENS // V7X_SUBLANES - 1
    levels = _scan_levels(LRU_TS)

    def tile(j):
        return (n_tiles - 1 - j) if reverse else j

    def slab(rows):
        return pltpu.VMEM((LRU_SLABS, rows, V7X_LANES), F32)

    vec = lambda: _const_spec((1, LRU_WIDTH))
    scratch = [slab(V7X_SUBLANES), slab(LRU_TS + 2 * V7X_SUBLANES)]
    scratch += [slab(r) for r in levels] * 2
    scratch += [slab(r + 2 * V7X_SUBLANES) for r in levels[1:]]
    return pl.pallas_call(
        functools.partial(_lru_kernel, reverse=reverse, n_tiles=n_tiles),
        grid=(BATCH, n_tiles),
        in_specs=[
            pl.BlockSpec((LRU_SLABS, LRU_TS, V7X_LANES), lambda b, j: (0, b * n_tiles + tile(j), 0)),
            pl.BlockSpec((LRU_SLABS, V7X_SUBLANES, V7X_LANES),
                         lambda b, j: (0, jnp.maximum((b * n_tiles + tile(j)) * sub - 1, 0), 0)),
            pl.BlockSpec((LRU_SLABS, V7X_SUBLANES, V7X_LANES),
                         lambda b, j: (0, jnp.minimum((b * n_tiles + tile(j) + 1) * sub, last8), 0)),
            pl.BlockSpec((LRU_SLABS, CTX_LEN, V7X_LANES), lambda b, j: (0, b, 0)),
            _const_spec((4, LRU_WIDTH)),
            vec(),
            _const_spec((2, LRU_WIDTH // 2, LRU_WIDTH)),
            vec(), vec(), vec(),
        ],
        out_specs=pl.BlockSpec((LRU_SLABS, LRU_TS, V7X_LANES), lambda b, j: (0, b * n_tiles + tile(j), 0)),
        out_shape=jax.ShapeDtypeStruct((LRU_SLABS, TOKENS, V7X_LANES), F32),
        scratch_shapes=scratch,
        compiler_params=_params(40 * 1024 * 1024, 2),
        name="lru_bwd" if reverse else "lru_fwd",
    )(xr, xr, xr, xr_ctx, conv_w, conv_b, wg_half, b_r_half, b_i_half, lam)


def _lru_gate_weights(w_r, w_i):
    def half_diag(w, hf):
        out = jnp.zeros((LRU_WIDTH // 2, LRU_WIDTH // 2), F32)
        for g in range(LRU_BLOCKS // 2):
            lo = g * LRU_BLOCK
            out = out.at[lo:lo + LRU_BLOCK, lo:lo + LRU_BLOCK].set(w[hf * (LRU_BLOCKS // 2) + g])
        return out
    halves = [jnp.concatenate([half_diag(w_r, hf), half_diag(w_i, hf)], axis=1) for hf in range(2)]
    return jnp.stack(halves).astype(BF16)


def _ffn_tail(h, mod_ref, g2_ref, wgu_ref, wdown_ref):
    n2 = _norm_mod(h, g2_ref[...], mod_ref[3:4, :], mod_ref[4:5, :]).astype(BF16)
    acc = jnp.zeros(h.shape, F32)
    for c0, width in FFN_CHUNKS:
        gt = jnp.dot(n2, wgu_ref[:, c0:c0 + width], preferred_element_type=F32)
        up = jnp.dot(n2, wgu_ref[:, FFN_HIDDEN + c0:FFN_HIDDEN + c0 + width], preferred_element_type=F32)
        act = (gt * _sigmoid(gt) * up).astype(BF16)
        acc = acc + jnp.dot(act, wdown_ref[c0:c0 + width, :], preferred_element_type=F32)
    return h + mod_ref[5:6, :] * acc


def _post0_kernel(x_ref, att_ref, hf_ref, hb_ref, gr_ref, mod_ref, g2_ref, wout_ref, wgu_ref, wdown_ref, o_ref):
    lat = jnp.concatenate([hf_ref[s] + hb_ref[s] for s in range(LRU_SLABS)], axis=1)
    rec = (lat * _gelu_tanh(gr_ref[...].astype(F32))).astype(BF16)
    y = (jnp.dot(att_ref[...], wout_ref[:ATT_WIDTH, :], preferred_element_type=F32)
         + jnp.dot(rec, wout_ref[ATT_WIDTH:, :], preferred_element_type=F32))
    h = x_ref[...] + mod_ref[2:3, :] * y
    o_ref[...] = _ffn_tail(h, mod_ref, g2_ref, wgu_ref, wdown_ref)


def _post0(x2d, att, h_fwd, h_bwd, gr, mod, g2, w_out, w_gu, w_down):
    tiles_per_sample = SEQ // TM
    tok = lambda w: pl.BlockSpec((TM, w), lambda i: (i, 0))
    slabs = lambda: pl.BlockSpec((LRU_SLABS, TM, V7X_LANES), lambda i: (0, i, 0))
    return pl.pallas_call(
        _post0_kernel,
        grid=(TOKENS // TM,),
        in_specs=[
            tok(D_MODEL), tok(ATT_WIDTH), slabs(), slabs(), tok(LRU_WIDTH),
            pl.BlockSpec((None, 6, D_MODEL), lambda i: (i // tiles_per_sample, 0, 0)),
            _const_spec((1, D_MODEL)),
            _const_spec((D_MODEL, D_MODEL)),
            _const_spec((D_MODEL, 2 * FFN_HIDDEN)),
            _const_spec((FFN_HIDDEN, D_MODEL)),
        ],
        out_specs=tok(D_MODEL),
        out_shape=jax.ShapeDtypeStruct((TOKENS, D_MODEL), F32),
        compiler_params=_params(56 * 1024 * 1024, 1),
        name="post0",
    )(x2d, att, h_fwd, h_bwd, gr, mod, g2, w_out, w_gu, w_down)


def _layer1_kernel(h_ref, hprev_ref, hnext_ref, mod_ref, g1_ref, win_ref, cw_ref, wout_ref,
                   g2_ref, wgu_ref, wdown_ref, gf_ref, o_ref, *, tiles_per_sample):
    tt = pl.program_id(0) % tiles_per_sample
    h = h_ref[...]
    h_full = jnp.concatenate([hprev_ref[...], h, hnext_ref[...]], axis=0)
    n_full = h_full.shape[0]
    body = slice(V7X_SUBLANES, V7X_SUBLANES + TM)
    n = _norm_mod(h_full, g1_ref[...], mod_ref[0:1, :], mod_ref[1:2, :]).astype(BF16)
    cx = jnp.dot(n, win_ref[:, D_MODEL:], preferred_element_type=F32)
    prod = cx[:, :D_MODEL] * cx[:, D_MODEL:]
    row = lax.broadcasted_iota(jnp.int32, (n_full, D_MODEL), 0)
    outside = ((row < V7X_SUBLANES) & (tt == 0)) | ((row >= V7X_SUBLANES + TM) & (tt == tiles_per_sample - 1))
    prod = jnp.where(outside, 0.0, prod)
    conv = (cw_ref[0:1, :] * pltpu.roll(prod, 1, 0)[body]
            + cw_ref[1:2, :] * prod[body]
            + cw_ref[2:3, :] * pltpu.roll(prod, n_full - 1, 0)[body])
    bg = jnp.dot(n[body], win_ref[:, :D_MODEL], preferred_element_type=F32)
    y = jnp.dot((bg * conv).astype(BF16), wout_ref[...], preferred_element_type=F32)
    h1 = h + mod_ref[2:3, :] * y
    h2 = _ffn_tail(h1, mod_ref, g2_ref, wgu_ref, wdown_ref)
    ms = jnp.mean(h2 * h2, axis=-1, keepdims=True)
    o_ref[...] = h2 * lax.rsqrt(ms + EPS) * gf_ref[...]


def _layer1(h, mod, g1, w_in, conv_w, w_out, g2, w_gu, w_down, g_final):
    tiles_per_sample = SEQ // TM
    sub = TM // V7X_SUBLANES
    last8 = TOKENS // V7X_SUBLANES - 1
    return pl.pallas_call(
        functools.partial(_layer1_kernel, tiles_per_sample=tiles_per_sample),
        grid=(TOKENS // TM,),
        in_specs=[
            pl.BlockSpec((TM, D_MODEL), lambda i: (i, 0)),
            pl.BlockSpec((V7X_SUBLANES, D_MODEL), lambda i: (jnp.maximum(i * sub - 1, 0), 0)),
            pl.BlockSpec((V7X_SUBLANES, D_MODEL), lambda i: (jnp.minimum((i + 1) * sub, last8), 0)),
            pl.BlockSpec((None, 6, D_MODEL), lambda i: (i // tiles_per_sample, 0, 0)),
            _const_spec((1, D_MODEL)),
            _const_spec((D_MODEL, 3 * D_MODEL)),
            _const_spec((3, D_MODEL)),
            _const_spec((D_MODEL, D_MODEL)),
            _const_spec((1, D_MODEL)),
            _const_spec((D_MODEL, 2 * FFN_HIDDEN)),
            _const_spec((FFN_HIDDEN, D_MODEL)),
            _const_spec((1, D_MODEL)),
        ],
        out_specs=pl.BlockSpec((TM, D_MODEL), lambda i: (i, 0)),
        out_shape=jax.ShapeDtypeStruct((TOKENS, D_MODEL), F32),
        compiler_params=_params(58 * 1024 * 1024, 1),
        name="layer1",
    )(h, h, h, mod, g1, w_in, conv_w, w_out, g2, w_gu, w_down, g_final)


def kernel(x, c, ctx, c_ctx, mod_w, mod_b, norm1_g, norm2_g, ffn_w_gu, ffn_w_down, e_w_in, e_w_out, e_conv_w, e_conv_b, e_lru_wr, e_lru_br, e_lru_wi, e_lru_bi, e_lru_lam, e_rpb, o_w_in, o_conv_w, o_w_out, final_g):
    depth = mod_w.shape[0]
    x2d = x.reshape(TOKENS, D_MODEL)
    ctx2d = ctx.reshape(BATCH * CTX_LEN, D_MODEL)

    cond = jnp.zeros((COND_ROWS, D_MODEL), F32).at[:BATCH].set(c).at[CTX_ROW].set(c_ctx)
    mods = _adaln(cond, mod_w, mod_b).reshape(depth, COND_ROWS, 6, D_MODEL)

    w_in = e_w_in[0]
    wqT = (w_in[:, :ATT_WIDTH].T * (HEAD_DIM ** -0.5 * LOG2E)).astype(BF16)
    wvT = w_in[:, 2 * ATT_WIDTH:3 * ATT_WIDTH].T.astype(BF16)
    wkxg = jnp.concatenate([w_in[:, ATT_WIDTH:2 * ATT_WIDTH], w_in[:, 3 * ATT_WIDTH:]], axis=1).astype(BF16)
    g1 = norm1_g[0].reshape(1, D_MODEL)

    tiles_per_sample = SEQ // TM
    qT, k, vT, xr, gr = _inproj0(x2d, mods[0], g1, wqT, wvT, wkxg, TM, lambda i: i // tiles_per_sample)
    _, k_ctx, vT_ctx, xr_ctx, _ = _inproj0(ctx2d, mods[0], g1, wqT, wvT, wkxg, CTX_LEN, lambda i: CTX_ROW)

    att = _attention(qT, k, vT, k_ctx, vT_ctx, _attention_bias(e_rpb[0]))

    conv_w = e_conv_w[0]
    conv_b = e_conv_b[0].reshape(1, LRU_WIDTH)
    lat = []
    for d, reverse in enumerate((False, True)):
        wg_half = _lru_gate_weights(0.5 * e_lru_wr[0, d], 0.5 * e_lru_wi[0, d])
        lat.append(_lru(xr, xr_ctx, conv_w, conv_b, wg_half,
                        0.5 * e_lru_br[0, d].reshape(1, LRU_WIDTH), 0.5 * e_lru_bi[0, d].reshape(1, LRU_WIDTH),
                        e_lru_lam[0, d].reshape(1, LRU_WIDTH), reverse))

    h = _post0(x2d, att, lat[0], lat[1], gr, mods[0], norm2_g[0].reshape(1, D_MODEL),
               e_w_out[0].astype(BF16), ffn_w_gu[0].astype(BF16), ffn_w_down[0].astype(BF16))

    out = _layer1(h, mods[1], norm1_g[1].reshape(1, D_MODEL), o_w_in[0].astype(BF16), o_conv_w[0],
                  o_w_out[0].astype(BF16), norm2_g[1].reshape(1, D_MODEL),
                  ffn_w_gu[1].astype(BF16), ffn_w_down[1].astype(BF16), final_g.reshape(1, D_MODEL))
    return out.reshape(BATCH, SEQ, D_MODEL)
```

```python
import functools

import numpy as np
import jax
import jax.numpy as jnp
from jax import lax
from jax.experimental import pallas as pl
from jax.experimental.pallas import tpu as pltpu

F32 = jnp.float32
BF16 = jnp.bfloat16

D_MODEL = 1024
BATCH = 2
SEQ = 16384
TOKENS = BATCH * SEQ
GRID_W = 64
GRID_ROWS = SEQ // GRID_W
CTX_LEN = 256
HEAD_DIM = 64
ATT_WIDTH = 512
ATT_HEADS = ATT_WIDTH // HEAD_DIM
HEAD_PAIRS = ATT_HEADS // 2
LRU_WIDTH = 512
LRU_BLOCKS = 8
LRU_BLOCK = LRU_WIDTH // LRU_BLOCKS
LRU_C = 8.0
WIN_ROWS = 8
WIN_COLS = 16
FFN_HIDDEN = 2816
EPS = 1e-6

V7X_SUBLANES = 8
V7X_LANES = 128
V7X_MXU_DIM = 256
V7X_VMEM_BYTES = 64 * 1024 * 1024

Q_ROWS = 4
Q_BLK = Q_ROWS * GRID_W
N_QBLK = GRID_ROWS // Q_ROWS
WIN_KEYS = 3 * Q_BLK
MASK_VALUE = -1e30

TM = 512
LRU_TS = 1024
LRU_SLABS = LRU_WIDTH // V7X_LANES
LRU_CHUNK = 256
SCAN_SEG = 4
SCAN_GROUP = V7X_SUBLANES * SCAN_SEG
SCAN_TOP = 16
LOG2E = 1.4426950408889634
F32_TINY = float(np.finfo(np.float32).tiny)
FFN_CHUNKS = ((0, 512), (512, 512), (1024, 512), (1536, 512), (2048, 512), (2560, 256))
COND_ROWS = 8
CTX_ROW = BATCH

_NT_DIMS = (((1,), (1,)), ((), ()))


def _const_spec(shape):
    nd = len(shape)
    return pl.BlockSpec(shape, lambda *_: (0,) * nd, pipeline_mode=pl.Buffered(1))


def _params(vmem_bytes, ndims):
    return pltpu.CompilerParams(
        dimension_semantics=("arbitrary",) * ndims,
        vmem_limit_bytes=min(int(vmem_bytes), V7X_VMEM_BYTES - 6 * 1024 * 1024),
    )


def _norm_mod(x, gain, shift, scale):
    ms = jnp.mean(x * x, axis=-1, keepdims=True)
    y = x * lax.rsqrt(ms + EPS) * gain
    return y * (1.0 + scale) + shift


def _sigmoid(x):
    return 0.5 * jnp.tanh(0.5 * x) + 0.5


def _gelu_tanh(x):
    return 0.5 * x * (1.0 + jnp.tanh(0.7978845608028654 * (x + 0.044715 * (x * x * x))))


def _adaln_kernel(c_ref, w_ref, b_ref, o_ref):
    c = c_ref[...]
    s = c * _sigmoid(c)
    o_ref[...] = jnp.dot(s, w_ref[...], preferred_element_type=F32,
                         precision=lax.Precision.HIGHEST) + b_ref[...]


def _adaln(cond, mod_w, mod_b):
    depth = mod_w.shape[0]
    n_out = mod_w.shape[2]
    tn = 1536
    return pl.pallas_call(
        _adaln_kernel,
        grid=(depth, n_out // tn),
        in_specs=[
            pl.BlockSpec((COND_ROWS, D_MODEL), lambda l, j: (0, 0)),
            pl.BlockSpec((None, D_MODEL, tn), lambda l, j: (l, 0, j)),
            pl.BlockSpec((None, 1, tn), lambda l, j: (l, 0, j)),
        ],
        out_specs=pl.BlockSpec((None, COND_ROWS, tn), lambda l, j: (l, 0, j)),
        out_shape=jax.ShapeDtypeStruct((depth, COND_ROWS, n_out), F32),
        compiler_params=_params(3 * D_MODEL * tn * 4, 2),
        name="adaln",
    )(cond, mod_w, mod_b.reshape(depth, 1, n_out))


def _inproj0_kernel(x_ref, mod_ref, g_ref, wqT_ref, wvT_ref, wkxg_ref,
                    qT_ref, k_ref, vT_ref, xr_ref, gr_ref):
    n = _norm_mod(x_ref[...], g_ref[...], mod_ref[0:1, :], mod_ref[1:2, :]).astype(BF16)
    qT_ref[...] = lax.dot_general(wqT_ref[...], n, _NT_DIMS, preferred_element_type=F32).astype(BF16)
    vT_ref[...] = lax.dot_general(wvT_ref[...], n, _NT_DIMS, preferred_element_type=F32).astype(BF16)
    kxg = jnp.dot(n, wkxg_ref[...], preferred_element_type=F32)
    for s in range(HEAD_PAIRS):
        k_ref[s] = kxg[:, s * V7X_LANES:(s + 1) * V7X_LANES].astype(BF16)
    for s in range(LRU_SLABS):
        lo = ATT_WIDTH + s * V7X_LANES
        xr_ref[s] = kxg[:, lo:lo + V7X_LANES]
    gr_ref[...] = kxg[:, ATT_WIDTH + LRU_WIDTH:].astype(BF16)


def _inproj0(x2d, mod, gain, wqT, wvT, wkxg, tm, mod_row_of_tile):
    n_tok = x2d.shape[0]
    wide = ATT_WIDTH + 2 * LRU_WIDTH
    return pl.pallas_call(
        _inproj0_kernel,
        grid=(n_tok // tm,),
        in_specs=[
            pl.BlockSpec((tm, D_MODEL), lambda i: (i, 0)),
            pl.BlockSpec((None, 6, D_MODEL), lambda i: (mod_row_of_tile(i), 0, 0)),
            _const_spec((1, D_MODEL)),
            _const_spec((ATT_WIDTH, D_MODEL)),
            _const_spec((ATT_WIDTH, D_MODEL)),
            _const_spec((D_MODEL, wide)),
        ],
        out_specs=[
            pl.BlockSpec((ATT_WIDTH, tm), lambda i: (0, i)),
            pl.BlockSpec((HEAD_PAIRS, tm, V7X_LANES), lambda i: (0, i, 0)),
            pl.BlockSpec((ATT_WIDTH, tm), lambda i: (0, i)),
            pl.BlockSpec((LRU_SLABS, tm, V7X_LANES), lambda i: (0, i, 0)),
            pl.BlockSpec((tm, LRU_WIDTH), lambda i: (i, 0)),
        ],
        out_shape=[
            jax.ShapeDtypeStruct((ATT_WIDTH, n_tok), BF16),
            jax.ShapeDtypeStruct((HEAD_PAIRS, n_tok, V7X_LANES), BF16),
            jax.ShapeDtypeStruct((ATT_WIDTH, n_tok), BF16),
            jax.ShapeDtypeStruct((LRU_SLABS, n_tok, V7X_LANES), F32),
            jax.ShapeDtypeStruct((n_tok, LRU_WIDTH), BF16),
        ],
        compiler_params=_params(40 * 1024 * 1024, 1),
        name="inproj0",
    )(x2d, mod, gain, wqT, wvT, wkxg)


def _attention_kernel(zero_ref, qT_ref, kp_ref, kc_ref, kn_ref, vTp_ref, vTc_ref, vTn_ref,
                      kx_ref, vTx_ref, bias_ref, o_ref, s0_ref, s1_ref, m0_ref, m1_ref):
    s_refs, m_refs = (s0_ref, s1_ref), (m0_ref, m1_ref)
    k_refs = (kp_ref, kc_ref, kn_ref, kx_ref)
    vT_refs = (vTp_ref, vTc_ref, vTn_ref, vTx_ref)
    n_pieces = len(k_refs)
    slab = 2 * HEAD_DIM
    row = lax.broadcasted_iota(jnp.int32, (slab, Q_BLK), 0)
    ones = jnp.ones((2 * V7X_SUBLANES, Q_BLK), BF16)
    zero = zero_ref[0]

    def piece_rows(p):
        return pl.ds(pl.multiple_of(zero + p * Q_BLK, Q_BLK), Q_BLK)

    def scores(hp, slot):
        s_ref, m_ref = s_refs[slot], m_refs[slot]
        q_slab = qT_ref[hp * slab:(hp + 1) * slab, :]
        for e in range(2):
            keep = (row < HEAD_DIM) if e == 0 else (row >= HEAD_DIM)
            qh = jnp.where(keep, q_slab, jnp.zeros_like(q_slab))
            m = None
            for p in range(n_pieces):
                s = jnp.dot(k_refs[p][hp], qh, preferred_element_type=F32)
                if p < 3:
                    s = s + bias_ref[2 * hp + e, p * Q_BLK:(p + 1) * Q_BLK, :]
                s_ref[e, piece_rows(p), :] = s
                pm = jnp.max(s, axis=0, keepdims=True)
                m = pm if m is None else jnp.maximum(m, pm)
            m_ref[e] = jnp.broadcast_to(m, (V7X_SUBLANES, Q_BLK))

    def outputs(hp, slot):
        s_ref, m_ref = s_refs[slot], m_refs[slot]
        pair = []
        for e in range(2):
            m = m_ref[e, 0:1, :]
            rows = slice((2 * hp + e) * HEAD_DIM, (2 * hp + e + 1) * HEAD_DIM)
            acc = None
            for p in range(n_pieces):
                w = jnp.exp2(s_ref[e, piece_rows(p), :] - m).astype(BF16)
                lhs = jnp.concatenate([vT_refs[p][rows, :], ones], axis=0)
                part = jnp.dot(lhs, w, preferred_element_type=F32)
                acc = part if acc is None else acc + part
            pair.append(acc[:HEAD_DIM] / acc[HEAD_DIM:HEAD_DIM + 1])
        o_ref[hp] = jnp.concatenate(pair, axis=0).T.astype(BF16)

    for t in range(HEAD_PAIRS + 1):
        if t < HEAD_PAIRS:
            scores(t, t % 2)
        if t >= 1:
            outputs(t - 1, (t - 1) % 2)


def _attention_bias(rpb):
    n_dr, n_dc = 2 * WIN_ROWS - 1, 2 * WIN_COLS - 1
    kcol = np.arange(GRID_W)[:, None]
    qcol = np.arange(GRID_W)[None, :]
    col_start = np.clip(qcol - WIN_COLS // 2, 0, GRID_W - WIN_COLS)
    col_ok = (kcol >= col_start) & (kcol < col_start + WIN_COLS)
    dc = kcol - qcol + (WIN_COLS - 1)
    col_sel = ((dc[None] == np.arange(n_dc)[:, None, None]) & col_ok[None]).astype(np.float32)
    band = jnp.einsum("hrd,dkq->hrkq", rpb.astype(F32) * LOG2E, col_sel, precision=lax.Precision.HIGHEST)
    band = jnp.where(col_ok[None, None], band, MASK_VALUE)
    band = jnp.concatenate([band, jnp.full((ATT_HEADS, 1, GRID_W, GRID_W), MASK_VALUE, F32)], axis=1)
    patterns = []
    for i in (0, 1, N_QBLK - 1):
        key_rows = []
        for kr in range(3 * Q_ROWS):
            blk = i - 1 + kr // Q_ROWS
            krow = Q_ROWS * blk + kr % Q_ROWS
            blocks = []
            for qr in range(Q_ROWS):
                qrow = Q_ROWS * i + qr
                row_start = min(max(qrow - WIN_ROWS // 2, 0), GRID_ROWS - WIN_ROWS)
                ok = 0 <= blk < N_QBLK and row_start <= krow < row_start + WIN_ROWS
                blocks.append(band[:, (krow - qrow + WIN_ROWS - 1) if ok else n_dr])
            key_rows.append(jnp.concatenate(blocks, axis=2))
        patterns.append(jnp.concatenate(key_rows, axis=1))
    return jnp.stack(patterns)


def _attention(qT, k, vT, k_ctx, vT_ctx, bias):
    def blk(b, i):
        return b * N_QBLK + i

    def pattern(i):
        return jnp.where(i == 0, 0, jnp.where(i == N_QBLK - 1, 2, 1))

    k_spec = lambda f: pl.BlockSpec((HEAD_PAIRS, Q_BLK, V7X_LANES), lambda b, i: (0, blk(b, f(i)), 0))
    vT_spec = lambda f: pl.BlockSpec((ATT_WIDTH, Q_BLK), lambda b, i: (0, blk(b, f(i))))
    prev = lambda i: jnp.maximum(i - 1, 0)
    cur = lambda i: i
    nxt = lambda i: jnp.minimum(i + 1, N_QBLK - 1)
    return pl.pallas_call(
        _attention_kernel,
        grid=(BATCH, N_QBLK),
        in_specs=[
            pl.BlockSpec(memory_space=pltpu.SMEM),
            vT_spec(cur),
            k_spec(prev), k_spec(cur), k_spec(nxt),
            vT_spec(prev), vT_spec(cur), vT_spec(nxt),
            pl.BlockSpec((HEAD_PAIRS, CTX_LEN, V7X_LANES), lambda b, i: (0, b, 0)),
            pl.BlockSpec((ATT_WIDTH, CTX_LEN), lambda b, i: (0, b)),
            pl.BlockSpec((None, ATT_HEADS, WIN_KEYS, Q_BLK), lambda b, i: (pattern(i), 0, 0, 0)),
        ],
        out_specs=pl.BlockSpec((HEAD_PAIRS, Q_BLK, V7X_LANES), lambda b, i: (0, blk(b, i), 0)),
        out_shape=jax.ShapeDtypeStruct((HEAD_PAIRS, TOKENS, V7X_LANES), BF16),
        scratch_shapes=[pltpu.VMEM((2, WIN_KEYS + CTX_LEN, Q_BLK), F32)] * 2
                       + [pltpu.VMEM((2, V7X_SUBLANES, Q_BLK), F32)] * 2,
        compiler_params=_params(40 * 1024 * 1024, 2),
        name="attention",
    )(jnp.zeros((1,), jnp.int32), qT, k, k, k, vT, vT, vT, k_ctx, vT_ctx, bias)


def _scan_levels(n):
    rows = [n]
    while rows[-1] > SCAN_TOP:
        assert rows[-1] % SCAN_GROUP == 0
        rows.append(rows[-1] // SCAN_SEG)
    return rows


def _lru_gates(xbuf, n, a_ref, u_ref, conv_w, conv_b, wg_ref, b_r_half, b_i_half, c_half):
    half = LRU_WIDTH // 2
    for c0 in range(0, n, LRU_CHUNK):
        xc = []
        for s in range(LRU_SLABS):
            ln = slice(s * V7X_LANES, (s + 1) * V7X_LANES)
            acc = conv_b[:, ln]
            for k in range(4):
                lo = c0 + V7X_SUBLANES - 2 + k
                acc = acc + conv_w[k:k + 1, ln] * xbuf[s, lo:lo + LRU_CHUNK, :]
            xc.append(acc)
        g = [jnp.dot(jnp.concatenate(xc[2 * hf:2 * hf + 2], axis=1).astype(BF16), wg_ref[hf],
                     preferred_element_type=F32) for hf in range(2)]
        for s in range(LRU_SLABS):
            hf, q = divmod(s, 2)
            ln = slice(s * V7X_LANES, (s + 1) * V7X_LANES)
            zr = g[hf][:, q * V7X_LANES:(q + 1) * V7X_LANES] + b_r_half[:, ln]
            zi = g[hf][:, half + q * V7X_LANES:half + (q + 1) * V7X_LANES] + b_i_half[:, ln]
            log_a = c_half[:, ln] * jnp.tanh(zr) + c_half[:, ln]
            th = jnp.tanh(log_a)
            m2 = (-2.0 * th) / (1.0 - th)
            mult = m2 * lax.rsqrt(jnp.maximum(m2, F32_TINY))
            ix = (0.5 * jnp.tanh(zi) + 0.5) * xc[s]
            a_ref[s, c0:c0 + LRU_CHUNK, :] = jnp.exp(log_a)
            u_ref[s, c0:c0 + LRU_CHUNK, :] = mult * ix


def _lru_scan(lvl_a, lvl_u, lvl_t, out_ref, n, carry_in, reverse):
    rows = _scan_levels(n)
    top = len(rows) - 1
    order = tuple(reversed(range(SCAN_SEG))) if reverse else tuple(range(SCAN_SEG))
    t_off = (V7X_SUBLANES - 1) if reverse else (V7X_SUBLANES + 1)

    def seg_loads(ref, s, base):
        return [ref[s, pl.ds(base + j, V7X_SUBLANES, stride=SCAN_SEG), :] for j in range(SCAN_SEG)]

    for k in range(top):
        def up(g, carry, k=k):
            base = g * SCAN_GROUP
            dst = pl.multiple_of(g * V7X_SUBLANES, V7X_SUBLANES)
            for s in range(LRU_SLABS):
                a = seg_loads(lvl_a[k], s, base)
                u = seg_loads(lvl_u[k], s, base)
                p, h = a[order[0]], u[order[0]]
                for j in order[1:]:
                    h = a[j] * h + u[j]
                    p = a[j] * p
                lvl_a[k + 1][s, pl.ds(dst, V7X_SUBLANES), :] = p
                lvl_u[k + 1][s, pl.ds(dst, V7X_SUBLANES), :] = h
            return carry
        lax.fori_loop(0, rows[k] // SCAN_GROUP, up, 0, unroll=2)

    carry_out = []
    for s in range(LRU_SLABS):
        h = carry_in[s]
        for r in (reversed(range(rows[top])) if reverse else range(rows[top])):
            h = lvl_a[top][s, r:r + 1, :] * h + lvl_u[top][s, r:r + 1, :]
            if out_ref is not None:
                lvl_t[top][s, t_off + r:t_off + r + 1, :] = h
        carry_out.append(h)
    if out_ref is None:
        return carry_out

    for k in reversed(range(top)):
        c_row = (rows[k + 1] + V7X_SUBLANES - 1) if reverse else V7X_SUBLANES
        for s in range(LRU_SLABS):
            lvl_t[k + 1][s, c_row:c_row + 1, :] = carry_in[s]

        def down(g, carry, k=k):
            base = g * SCAN_GROUP
            src = pl.multiple_of((g + 1) * V7X_SUBLANES, V7X_SUBLANES)
            for s in range(LRU_SLABS):
                a = seg_loads(lvl_a[k], s, base)
                u = seg_loads(lvl_u[k], s, base)
                h = lvl_t[k + 1][s, pl.ds(src, V7X_SUBLANES), :]
                for j in order:
                    h = a[j] * h + u[j]
                    if k == 0:
                        out_ref[s, pl.ds(base + j, V7X_SUBLANES, stride=SCAN_SEG), :] = h
                    else:
                        lvl_t[k][s, pl.ds(t_off + base + j, V7X_SUBLANES, stride=SCAN_SEG), :] = h
            return carry
        lax.fori_loop(0, rows[k] // SCAN_GROUP, down, 0, unroll=2)
    return carry_out


def _lru_kernel(x_ref, xprev_ref, xnext_ref, xctx_ref, cw_ref, cb_ref, wg_ref, br_ref, bi_ref, lam_ref,
                o_ref, carry_ref, xbuf, *lvl, reverse, n_tiles):
    n_lvl = len(_scan_levels(LRU_TS))
    lvl_a, lvl_u, lvl_t = lvl[:n_lvl], lvl[n_lvl:2 * n_lvl], (None,) + tuple(lvl[2 * n_lvl:])
    j = pl.program_id(1)
    t = (n_tiles - 1 - j) if reverse else j
    z = -lam_ref[...]
    sp = jnp.maximum(z, 0.0) + jnp.log1p(jnp.exp(-jnp.abs(z)))
    c_half = (-0.5 * LRU_C) * sp
    conv_w = cw_ref[...]
    conv_b = cb_ref[...]
    gate_args = (conv_w, conv_b, wg_ref, br_ref[...], bi_ref[...], c_half)
    zeros8 = jnp.zeros((V7X_SUBLANES, V7X_LANES), F32)
    body = V7X_SUBLANES

    @pl.when(j == 0)
    def _():
        for s in range(LRU_SLABS):
            xbuf[s, 0:body, :] = zeros8
            xbuf[s, body:body + CTX_LEN, :] = xctx_ref[s]
            xbuf[s, body + CTX_LEN:2 * body + CTX_LEN, :] = zeros8
        _lru_gates(xbuf, CTX_LEN, lvl_a[0], lvl_u[0], *gate_args)
        h0 = _lru_scan(lvl_a, lvl_u, lvl_t, None, CTX_LEN,
                       [jnp.zeros((1, V7X_LANES), F32)] * LRU_SLABS, reverse)
        for s in range(LRU_SLABS):
            carry_ref[s, 0:1, :] = h0[s]

    for s in range(LRU_SLABS):
        xbuf[s, 0:body, :] = jnp.where(t == 0, zeros8, xprev_ref[s])
        xbuf[s, body:body + LRU_TS, :] = x_ref[s]
        xbuf[s, body + LRU_TS:2 * body + LRU_TS, :] = jnp.where(t == n_tiles - 1, zeros8, xnext_ref[s])
    _lru_gates(xbuf, LRU_TS, lvl_a[0], lvl_u[0], *gate_args)
    h_out = _lru_scan(lvl_a, lvl_u, lvl_t, o_ref, LRU_TS,
                      [carry_ref[s, 0:1, :] for s in range(LRU_SLABS)], reverse)
    for s in range(LRU_SLABS):
        carry_ref[s, 0:1, :] = h_out[s]


def _lru(xr, xr_ctx, conv_w, conv_b, wg_half, b_r_half, b_i_half, lam, reverse):
    n_tiles = SEQ // LRU_TS
    sub = LRU_TS // V7X_SUBLANES
    last8 = TOKENS // V7X_SUBLANES - 1
    levels = _scan_levels(LRU_TS)

    def tile(j):
        return (n_tiles - 1 - j) if reverse else j

    def slab(rows):
        return pltpu.VMEM((LRU_SLABS, rows, V7X_LANES), F32)

    vec = lambda: _const_spec((1, LRU_WIDTH))
    scratch = [slab(V7X_SUBLANES), slab(LRU_TS + 2 * V7X_SUBLANES)]
    scratch += [slab(r) for r in levels] * 2
    scratch += [slab(r + 2 * V7X_SUBLANES) for r in levels[1:]]
    return pl.pallas_call(
        functools.partial(_lru_kernel, reverse=reverse, n_tiles=n_tiles),
        grid=(BATCH, n_tiles),
        in_specs=[
            pl.BlockSpec((LRU_SLABS, LRU_TS, V7X_LANES), lambda b, j: (0, b * n_tiles + tile(j), 0)),
            pl.BlockSpec((LRU_SLABS, V7X_SUBLANES, V7X_LANES),
                         lambda b, j: (0, jnp.maximum((b * n_tiles + tile(j)) * sub - 1, 0), 0)),
            pl.BlockSpec((LRU_SLABS, V7X_SUBLANES, V7X_LANES),
                         lambda b, j: (0, jnp.minimum((b * n_tiles + tile(j) + 1) * sub, last8), 0)),
            pl.BlockSpec((LRU_SLABS, CTX_LEN, V7X_LANES), lambda b, j: (0, b, 0)),
            _const_spec((4, LRU_WIDTH)),
            vec(),
            _const_spec((2, LRU_WIDTH // 2, LRU_WIDTH)),
            vec(), vec(), vec(),
        ],
        out_specs=pl.BlockSpec((LRU_SLABS, LRU_TS, V7X_LANES), lambda b, j: (0, b * n_tiles + tile(j), 0)),
        out_shape=jax.ShapeDtypeStruct((LRU_SLABS, TOKENS, V7X_LANES), F32),
        scratch_shapes=scratch,
        compiler_params=_params(40 * 1024 * 1024, 2),
        name="lru_bwd" if reverse else "lru_fwd",
    )(xr, xr, xr, xr_ctx, conv_w, conv_b, wg_half, b_r_half, b_i_half, lam)


def _lru_gate_weights(w_r, w_i):
    def half_diag(w, hf):
        out = jnp.zeros((LRU_WIDTH // 2, LRU_WIDTH // 2), F32)
        for g in range(LRU_BLOCKS // 2):
            lo = g * LRU_BLOCK
            out = out.at[lo:lo + LRU_BLOCK, lo:lo + LRU_BLOCK].set(w[hf * (LRU_BLOCKS // 2) + g])
        return out
    halves = [jnp.concatenate([half_diag(w_r, hf), half_diag(w_i, hf)], axis=1) for hf in range(2)]
    return jnp.stack(halves).astype(BF16)


def _ffn_tail(h, mod_ref, g2_ref, wgu_ref, wdown_ref):
    n2 = _norm_mod(h, g2_ref[...], mod_ref[3:4, :], mod_ref[4:5, :]).astype(BF16)
    acc = jnp.zeros(h.shape, F32)
    for c0, width in FFN_CHUNKS:
        gt = jnp.dot(n2, wgu_ref[:, c0:c0 + width], preferred_element_type=F32)
        up = jnp.dot(n2, wgu_ref[:, FFN_HIDDEN + c0:FFN_HIDDEN + c0 + width], preferred_element_type=F32)
        act = (gt * _sigmoid(gt) * up).astype(BF16)
        acc = acc + jnp.dot(act, wdown_ref[c0:c0 + width, :], preferred_element_type=F32)
    return h + mod_ref[5:6, :] * acc


def _post0_kernel(x_ref, att_ref, hf_ref, hb_ref, gr_ref, mod_ref, g2_ref, wout_ref, wgu_ref, wdown_ref, o_ref):
    lat = jnp.concatenate([hf_ref[s] + hb_ref[s] for s in range(LRU_SLABS)], axis=1)
    rec = (lat * _gelu_tanh(gr_ref[...].astype(F32))).astype(BF16)
    att = jnp.concatenate([att_ref[s] for s in range(HEAD_PAIRS)], axis=1)
    y = (jnp.dot(att, wout_ref[:ATT_WIDTH, :], preferred_element_type=F32)
         + jnp.dot(rec, wout_ref[ATT_WIDTH:, :], preferred_element_type=F32))
    h = x_ref[...] + mod_ref[2:3, :] * y
    o_ref[...] = _ffn_tail(h, mod_ref, g2_ref, wgu_ref, wdown_ref)


def _post0(x2d, att, h_fwd, h_bwd, gr, mod, g2, w_out, w_gu, w_down):
    tiles_per_sample = SEQ // TM
    tok = lambda w: pl.BlockSpec((TM, w), lambda i: (i, 0))
    slabs = lambda: pl.BlockSpec((LRU_SLABS, TM, V7X_LANES), lambda i: (0, i, 0))
    return pl.pallas_call(
        _post0_kernel,
        grid=(TOKENS // TM,),
        in_specs=[
            tok(D_MODEL), slabs(), slabs(), slabs(), tok(LRU_WIDTH),
            pl.BlockSpec((None, 6, D_MODEL), lambda i: (i // tiles_per_sample, 0, 0)),
            _const_spec((1, D_MODEL)),
            _const_spec((D_MODEL, D_MODEL)),
            _const_spec((D_MODEL, 2 * FFN_HIDDEN)),
            _const_spec((FFN_HIDDEN, D_MODEL)),
        ],
        out_specs=tok(D_MODEL),
        out_shape=jax.ShapeDtypeStruct((TOKENS, D_MODEL), F32),
        compiler_params=_params(56 * 1024 * 1024, 1),
        name="post0",
    )(x2d, att, h_fwd, h_bwd, gr, mod, g2, w_out, w_gu, w_down)


def _layer1_kernel(h_ref, hprev_ref, hnext_ref, mod_ref, g1_ref, win_ref, cw_ref, wout_ref,
                   g2_ref, wgu_ref, wdown_ref, gf_ref, o_ref, *, tiles_per_sample):
    tt = pl.program_id(0) % tiles_per_sample
    h = h_ref[...]
    h_full = jnp.concatenate([hprev_ref[...], h, hnext_ref[...]], axis=0)
    n_full = h_full.shape[0]
    body = slice(V7X_SUBLANES, V7X_SUBLANES + TM)
    n = _norm_mod(h_full, g1_ref[...], mod_ref[0:1, :], mod_ref[1:2, :]).astype(BF16)
    cx = jnp.dot(n, win_ref[:, D_MODEL:], preferred_element_type=F32)
    prod = cx[:, :D_MODEL] * cx[:, D_MODEL:]
    row = lax.broadcasted_iota(jnp.int32, (n_full, D_MODEL), 0)
    outside = ((row < V7X_SUBLANES) & (tt == 0)) | ((row >= V7X_SUBLANES + TM) & (tt == tiles_per_sample - 1))
    prod = jnp.where(outside, 0.0, prod)
    conv = (cw_ref[0:1, :] * pltpu.roll(prod, 1, 0)[body]
            + cw_ref[1:2, :] * prod[body]
            + cw_ref[2:3, :] * pltpu.roll(prod, n_full - 1, 0)[body])
    bg = jnp.dot(n[body], win_ref[:, :D_MODEL], preferred_element_type=F32)
    y = jnp.dot((bg * conv).astype(BF16), wout_ref[...], preferred_element_type=F32)
    h1 = h + mod_ref[2:3, :] * y
    h2 = _ffn_tail(h1, mod_ref, g2_ref, wgu_ref, wdown_ref)
    ms = jnp.mean(h2 * h2, axis=-1, keepdims=True)
    o_ref[...] = h2 * lax.rsqrt(ms + EPS) * gf_ref[...]


def _layer1(h, mod, g1, w_in, conv_w, w_out, g2, w_gu, w_down, g_final):
    tiles_per_sample = SEQ // TM
    sub = TM // V7X_SUBLANES
    last8 = TOKENS // V7X_SUBLANES - 1
    return pl.pallas_call(
        functools.partial(_layer1_kernel, tiles_per_sample=tiles_per_sample),
        grid=(TOKENS // TM,),
        in_specs=[
            pl.BlockSpec((TM, D_MODEL), lambda i: (i, 0)),
            pl.BlockSpec((V7X_SUBLANES, D_MODEL), lambda i: (jnp.maximum(i * sub - 1, 0), 0)),
            pl.BlockSpec((V7X_SUBLANES, D_MODEL), lambda i: (jnp.minimum((i + 1) * sub, last8), 0)),
            pl.BlockSpec((None, 6, D_MODEL), lambda i: (i // tiles_per_sample, 0, 0)),
            _const_spec((1, D_MODEL)),
            _const_spec((D_MODEL, 3 * D_MODEL)),
            _const_spec((3, D_MODEL)),
            _const_spec((D_MODEL, D_MODEL)),
            _const_spec((1, D_MODEL)),
            _const_spec((D_MODEL, 2 * FFN_HIDDEN)),
            _const_spec((FFN_HIDDEN, D_MODEL)),
            _const_spec((1, D_MODEL)),
        ],
        out_specs=pl.BlockSpec((TM, D_MODEL), lambda i: (i, 0)),
        out_shape=jax.ShapeDtypeStruct((TOKENS, D_MODEL), F32),
        compiler_params=_params(58 * 1024 * 1024, 1),
        name="layer1",
    )(h, h, h, mod, g1, w_in, conv_w, w_out, g2, w_gu, w_down, g_final)


def kernel(x, c, ctx, c_ctx, mod_w, mod_b, norm1_g, norm2_g, ffn_w_gu, ffn_w_down, e_w_in, e_w_out, e_conv_w, e_conv_b, e_lru_wr, e_lru_br, e_lru_wi, e_lru_bi, e_lru_lam, e_rpb, o_w_in, o_conv_w, o_w_out, final_g):
    depth = mod_w.shape[0]
    x2d = x.reshape(TOKENS, D_MODEL)
    ctx2d = ctx.reshape(BATCH * CTX_LEN, D_MODEL)

    cond = jnp.zeros((COND_ROWS, D_MODEL), F32).at[:BATCH].set(c).at[CTX_ROW].set(c_ctx)
    mods = _adaln(cond, mod_w, mod_b).reshape(depth, COND_ROWS, 6, D_MODEL)

    w_in = e_w_in[0]
    wqT = (w_in[:, :ATT_WIDTH].T * (HEAD_DIM ** -0.5 * LOG2E)).astype(BF16)
    wvT = w_in[:, 2 * ATT_WIDTH:3 * ATT_WIDTH].T.astype(BF16)
    wkxg = jnp.concatenate([w_in[:, ATT_WIDTH:2 * ATT_WIDTH], w_in[:, 3 * ATT_WIDTH:]], axis=1).astype(BF16)
    g1 = norm1_g[0].reshape(1, D_MODEL)

    tiles_per_sample = SEQ // TM
    qT, k, vT, xr, gr = _inproj0(x2d, mods[0], g1, wqT, wvT, wkxg, TM, lambda i: i // tiles_per_sample)
    _, k_ctx, vT_ctx, xr_ctx, _ = _inproj0(ctx2d, mods[0], g1, wqT, wvT, wkxg, CTX_LEN, lambda i: CTX_ROW)

    att = _attention(qT, k, vT, k_ctx, vT_ctx, _attention_bias(e_rpb[0]))

    conv_w = e_conv_w[0]
    conv_b = e_conv_b[0].reshape(1, LRU_WIDTH)
    lat = []
    for d, reverse in enumerate((False, True)):
        wg_half = _lru_gate_weights(0.5 * e_lru_wr[0, d], 0.5 * e_lru_wi[0, d])
        lat.append(_lru(xr, xr_ctx, conv_w, conv_b, wg_half,
                        0.5 * e_lru_br[0, d].reshape(1, LRU_WIDTH), 0.5 * e_lru_bi[0, d].reshape(1, LRU_WIDTH),
                        e_lru_lam[0, d].reshape(1, LRU_WIDTH), reverse))

    h = _post0(x2d, att, lat[0], lat[1], gr, mods[0], norm2_g[0].reshape(1, D_MODEL),
               e_w_out[0].astype(BF16), ffn_w_gu[0].astype(BF16), ffn_w_down[0].astype(BF16))

    out = _layer1(h, mods[1], norm1_g[1].reshape(1, D_MODEL), o_w_in[0].astype(BF16), o_conv_w[0],
                  o_w_out[0].astype(BF16), norm2_g[1].reshape(1, D_MODEL),
                  ffn_w_gu[1].astype(BF16), ffn_w_down[1].astype(BF16), final_g.reshape(1, D_MODEL))
    return out.reshape(BATCH, SEQ, D_MODEL)
```
